```python
import jax, jax.numpy as jnp
from jax import lax
import numpy as np

D_MODEL = 2048
BATCH = 4
SEQ = 4096
DEPTH = 1
DEC_BATCH = 32
DEC_SEQ = 1
PAST_LEN = 16384
PAGE_SIZE = 128

A_HEADS = 8
HEAD_DIM = 128
A_WIDTH = A_HEADS * HEAD_DIM
MOBA_BLOCK = 256
MOBA_TOPK = 3
Q_CHUNK = 64
B_GROUPS = 8
B_GROUP_DIM = 128
B_WIDTH = B_GROUPS * B_GROUP_DIM
SGU_CHUNK = 128
N_EXP_GROUPS = 4
EXP_PER_GROUP = 4
N_EXPERTS = N_EXP_GROUPS * EXP_PER_GROUP
D_EXPERT = 512
TOP_K_IN_GROUP = 2
NORM_EPS = 1e-6
NEG_INF = -1e30

IN_WIDTH = 3 * A_WIDTH + 2 * B_WIDTH + 2 * D_MODEL
SPLITS = (A_WIDTH, 2 * A_WIDTH, 3 * A_WIDTH, 3 * A_WIDTH + B_WIDTH,
          3 * A_WIDTH + 2 * B_WIDTH, 3 * A_WIDTH + 2 * B_WIDTH + D_MODEL)

kernel_name = 'hybrid_moba_sgu_hmoe_step'


def rms_norm(x, g):
    xf = x.astype(jnp.float32)
    inv = lax.rsqrt(jnp.mean(xf * xf, axis=-1, keepdims=True) + NORM_EPS)
    return (xf * inv).astype(x.dtype) * g


def alibi_slopes():
    return jnp.exp2(-8.0 * jnp.arange(1, A_HEADS + 1, dtype=jnp.float32) / A_HEADS)


def split_in(xn, w_in):
    z = jnp.matmul(xn, w_in)
    return jnp.split(z, list(SPLITS), axis=-1)


def moba_select(q, kmean, t_pos):
    n_blk = kmean.shape[1]
    if n_blk < MOBA_TOPK:
        kmean = jnp.pad(kmean, ((0, 0), (0, MOBA_TOPK - n_blk), (0, 0), (0, 0)))
    gate = jnp.einsum('bshd,bnhd->bshn', q.astype(jnp.float32), kmean.astype(jnp.float32))
    n_t = t_pos // MOBA_BLOCK
    fully_past = jnp.arange(kmean.shape[1])[None, :] < n_t[:, None]
    gate = jnp.where(fully_past[None, :, None, :], gate, NEG_INF)
    _, idx = lax.top_k(gate, MOBA_TOPK)
    return idx


def key_positions(idx, t_pos):
    n_t = t_pos // MOBA_BLOCK
    own = jnp.broadcast_to(n_t[:, None, None], idx.shape[:-1] + (1,)).astype(idx.dtype)
    blocks = jnp.concatenate([idx, own], axis=-1)
    s = blocks[..., None] * MOBA_BLOCK + jnp.arange(MOBA_BLOCK, dtype=idx.dtype)
    slot = jnp.arange(MOBA_TOPK + 1)[:, None]
    valid = jnp.where(slot < MOBA_TOPK, slot < n_t[:, None, None, None],
                      s <= t_pos[:, None, None, None])
    valid = jnp.broadcast_to(valid, s.shape)
    n_keys = (MOBA_TOPK + 1) * MOBA_BLOCK
    return s.reshape(s.shape[:-2] + (n_keys,)), valid.reshape(s.shape[:-2] + (n_keys,))


def moba_attend(q, t_pos, k_sel, v_sel, s_pos, valid, slopes):
    scores = jnp.einsum('...hd,...hkd->...hk', q, k_sel,
                        preferred_element_type=jnp.float32) * (HEAD_DIM ** -0.5)
    dist = (t_pos[:, None, None] - s_pos).astype(jnp.float32)
    scores = jnp.where(valid, scores - slopes[:, None] * dist, NEG_INF)
    p = jax.nn.softmax(scores, axis=-1)
    out = jnp.einsum('...hk,...hkd->...hd', p.astype(v_sel.dtype), v_sel,
                     preferred_element_type=jnp.float32)
    return out.astype(q.dtype)


def moba_prompt(q, k, v, slopes):
    b_sz, s_len = q.shape[:2]
    n_blk = -(-s_len // MOBA_BLOCK)
    pad = n_blk * MOBA_BLOCK - s_len
    k_pad = jnp.pad(k, ((0, 0), (0, pad), (0, 0), (0, 0)))
    v_pad = jnp.pad(v, ((0, 0), (0, pad), (0, 0), (0, 0)))
    kmean = jnp.mean(k_pad.reshape(b_sz, n_blk, MOBA_BLOCK, A_HEADS, HEAD_DIM), axis=2,
                     dtype=jnp.float32)
    t_pos = jnp.arange(s_len, dtype=jnp.int32)
    idx = moba_select(q, kmean, t_pos)
    n_qc = s_len // Q_CHUNK
    heads = jnp.arange(A_HEADS)[:, None]
    max_pos = n_blk * MOBA_BLOCK - 1

    def one_chunk(args):
        q_c, idx_c, b_c, t_c = args
        s_pos, valid = key_positions(idx_c, t_c)
        s_cl = jnp.clip(s_pos, 0, max_pos)
        k_sel = k_pad[b_c, s_cl, heads]
        v_sel = v_pad[b_c, s_cl, heads]
        return moba_attend(q_c, t_c, k_sel, v_sel, s_pos, valid, slopes)

    out = lax.map(one_chunk, (
        q.reshape(b_sz * n_qc, Q_CHUNK, A_HEADS, HEAD_DIM),
        idx.reshape(b_sz * n_qc, Q_CHUNK, A_HEADS, MOBA_TOPK),
        jnp.repeat(jnp.arange(b_sz, dtype=jnp.int32), n_qc),
        jnp.tile(t_pos.reshape(n_qc, Q_CHUNK), (b_sz, 1))))
    return out.reshape(b_sz, s_len, A_WIDTH)


def moba_sample(q, k_new, v_new, cache_k, cache_v, layer, page_table, page_sums, slopes):
    n_seq, n_new = q.shape[:2]
    n_pages = page_table.shape[1]
    past_len = n_pages * PAGE_SIZE
    pages_per_blk = MOBA_BLOCK // PAGE_SIZE
    t_pos = past_len + jnp.arange(n_new, dtype=jnp.int32)
    n_blk_tot = -(-(past_len + n_new) // MOBA_BLOCK)
    n_blk_past = -(-n_pages // pages_per_blk)
    ps = page_sums[page_table]
    ps = jnp.pad(ps, ((0, 0), (0, n_blk_past * pages_per_blk - n_pages), (0, 0), (0, 0)))
    blk_sum = ps.reshape(n_seq, n_blk_past, pages_per_blk, A_HEADS, HEAD_DIM).sum(axis=2)
    blk_sum = jnp.pad(blk_sum, ((0, 0), (0, n_blk_tot - n_blk_past), (0, 0), (0, 0)))
    onehot = (t_pos[:, None] // MOBA_BLOCK == jnp.arange(n_blk_tot)[None, :]).astype(jnp.float32)
    blk_sum = blk_sum + jnp.einsum('sn,bshd->bnhd', onehot, k_new.astype(jnp.float32))
    kmean = blk_sum / MOBA_BLOCK
    idx = moba_select(q, kmean, t_pos)
    s_pos, valid = key_positions(idx, t_pos)
    seqs = jnp.arange(n_seq)[:, None, None, None]
    heads = jnp.arange(A_HEADS)[None, None, :, None]
    in_past = (s_pos < past_len)[..., None]
    s_old = jnp.clip(s_pos, 0, past_len - 1)
    phys = page_table[seqs, s_old // PAGE_SIZE]
    off = s_old % PAGE_SIZE
    s_cur = jnp.clip(s_pos - past_len, 0, n_new - 1)
    k_sel = jnp.where(in_past, cache_k[layer, phys, off, heads], k_new[seqs, s_cur, heads])
    v_sel = jnp.where(in_past, cache_v[layer, phys, off, heads], v_new[seqs, s_cur, heads])
    out = moba_attend(q, t_pos, k_sel, v_sel, s_pos, valid, slopes)
    return out.reshape(n_seq, n_new, A_WIDTH)


def spatial_gating(u, vg, g, w_s, b_s):
    b_sz, s_len = u.shape[:2]
    vn = rms_norm(vg, g)
    n_c = -(-s_len // SGU_CHUNK)
    v_pad = jnp.pad(vn, ((0, 0), (0, n_c * SGU_CHUNK - s_len), (0, 0)))
    vc = v_pad.reshape(b_sz, n_c, SGU_CHUNK, B_GROUPS, B_GROUP_DIM)
    causal = jnp.tril(jnp.ones((SGU_CHUNK, SGU_CHUNK), dtype=bool))
    w = jnp.where(causal[None], w_s, 0.0)
    z = jnp.einsum('gij,bcjgd->bcigd', w, vc) + b_s.T[:, :, None]
    z = z.reshape(b_sz, n_c * SGU_CHUNK, B_WIDTH)[:, :s_len]
    return u * z, vn


def merge_out(att, sgu_out, ga, gb, w_pa, w_pb, w_o):
    mixed = jax.nn.sigmoid(ga) * jnp.matmul(att, w_pa) + jax.nn.sigmoid(gb) * jnp.matmul(sgu_out, w_pb)
    return jnp.matmul(mixed, w_o)


def hier_moe(xn, w_rg, b_rg, w_re, b_re, w_gate, w_up, w_down):
    lead = xn.shape[:-1]
    xt = xn.reshape(-1, D_MODEL)
    logit_g = jnp.matmul(xt, w_rg, preferred_element_type=jnp.float32) + b_rg
    grp = jnp.argmax(logit_g, axis=-1)
    p_grp = jnp.take_along_axis(jax.nn.softmax(logit_g, axis=-1), grp[:, None], axis=-1)
    logit_e = (jnp.matmul(xt, w_re, preferred_element_type=jnp.float32) + b_re).reshape(
        -1, N_EXP_GROUPS, EXP_PER_GROUP)
    logit_in = jnp.take_along_axis(logit_e, grp[:, None, None], axis=1)[:, 0]
    top_val, top_idx = lax.top_k(logit_in, TOP_K_IN_GROUP)
    w_top = jax.nn.softmax(top_val, axis=-1) * p_grp
    expert = grp[:, None] * EXP_PER_GROUP + top_idx
    combine = jnp.sum(jax.nn.one_hot(expert, N_EXPERTS, dtype=jnp.float32) * w_top[..., None], axis=1)
    h = jax.nn.silu(jnp.einsum('td,edf->tef', xt, w_gate)) * jnp.einsum('td,edf->tef', xt, w_up)
    y = jnp.einsum('tef,efd->td', h * combine[..., None].astype(h.dtype), w_down)
    return y.astype(xn.dtype).reshape(lead + (D_MODEL,))


def setup_inputs(seed: int = 0) -> dict:
    key = jax.random.key(seed)
    ks = jax.random.split(key, 24)
    nrm = jax.random.normal
    n_pages = PAST_LEN // PAGE_SIZE
    n_used = DEC_BATCH * n_pages
    n_pool = (5 * n_used + 3) // 4
    page_table = jax.random.permutation(ks[4], n_pool)[:n_used].reshape(DEC_BATCH, n_pages).astype(jnp.int32)
    return {
        'x_prompt': nrm(ks[0], (BATCH, SEQ, D_MODEL), jnp.float32),
        'x_sample': nrm(ks[1], (DEC_BATCH, DEC_SEQ, D_MODEL), jnp.float32),
        'cache_k': nrm(ks[2], (DEPTH, n_pool, PAGE_SIZE, A_HEADS, HEAD_DIM), jnp.float32),
        'cache_v': nrm(ks[3], (DEPTH, n_pool, PAGE_SIZE, A_HEADS, HEAD_DIM), jnp.float32),
        'page_table': page_table,
        'g_mix': 1.0 + 0.1 * nrm(ks[5], (DEPTH, D_MODEL), jnp.float32),
        'w_in': nrm(ks[6], (DEPTH, D_MODEL, IN_WIDTH), jnp.float32) * D_MODEL ** -0.5,
        'g_sgu': 1.0 + 0.1 * nrm(ks[7], (DEPTH, B_WIDTH), jnp.float32),
        'w_sgu': nrm(ks[8], (DEPTH, B_GROUPS, SGU_CHUNK, SGU_CHUNK), jnp.float32) * SGU_CHUNK ** -0.5,
        'b_sgu': 1.0 + 0.1 * nrm(ks[9], (DEPTH, B_GROUPS, SGU_CHUNK), jnp.float32),
        'w_proj_a': nrm(ks[10], (DEPTH, A_WIDTH, D_MODEL), jnp.float32) * A_WIDTH ** -0.5,
        'w_proj_b': nrm(ks[11], (DEPTH, B_WIDTH, D_MODEL), jnp.float32) * B_WIDTH ** -0.5,
        'w_out': nrm(ks[12], (DEPTH, D_MODEL, D_MODEL), jnp.float32) * D_MODEL ** -0.5,
        'g_ffn': 1.0 + 0.1 * nrm(ks[13], (DEPTH, D_MODEL), jnp.float32),
        'w_router_group': nrm(ks[14], (DEPTH, D_MODEL, N_EXP_GROUPS), jnp.float32) * D_MODEL ** -0.5,
        'b_router_group': 0.01 * nrm(ks[15], (DEPTH, N_EXP_GROUPS), jnp.float32),
        'w_router_expert': nrm(ks[16], (DEPTH, D_MODEL, N_EXPERTS), jnp.float32) * D_MODEL ** -0.5,
        'b_router_expert': 0.01 * nrm(ks[17], (DEPTH, N_EXPERTS), jnp.float32),
        'w_exp_gate': nrm(ks[18], (DEPTH, N_EXPERTS, D_MODEL, D_EXPERT), jnp.float32) * D_MODEL ** -0.5,
        'w_exp_up': nrm(ks[19], (DEPTH, N_EXPERTS, D_MODEL, D_EXPERT), jnp.float32) * D_MODEL ** -0.5,
        'w_exp_down': nrm(ks[20], (DEPTH, N_EXPERTS, D_EXPERT, D_MODEL), jnp.float32) * D_EXPERT ** -0.5,
        'g_final': 1.0 + 0.1 * nrm(ks[21], (D_MODEL,), jnp.float32),
    }


def reference(x_prompt, x_sample, cache_k, cache_v, page_table, g_mix, w_in, g_sgu, w_sgu, b_sgu,
              w_proj_a, w_proj_b, w_out, g_ffn, w_router_group, b_router_group,
              w_router_expert, b_router_expert, w_exp_gate, w_exp_up, w_exp_down, g_final):
    slopes = alibi_slopes()
    page_sums = jnp.sum(cache_k, axis=2, dtype=jnp.float32)
    xp, xs = x_prompt, x_sample
    k_prompt, v_prompt, k_sample, v_sample, sgu_sample = [], [], [], [], []
    for l in range(DEPTH):
        b_sz, s_len = xp.shape[:2]
        q, k, v, u, vg, ga, gb = split_in(rms_norm(xp, g_mix[l]), w_in[l])
        q = q.reshape(b_sz, s_len, A_HEADS, HEAD_DIM)
        k = k.reshape(b_sz, s_len, A_HEADS, HEAD_DIM)
        v = v.reshape(b_sz, s_len, A_HEADS, HEAD_DIM)
        att = moba_prompt(q, k, v, slopes)
        sgu_out, _ = spatial_gating(u, vg, g_sgu[l], w_sgu[l], b_sgu[l])
        xp = xp + merge_out(att, sgu_out, ga, gb, w_proj_a[l], w_proj_b[l], w_out[l])
        xp = xp + hier_moe(rms_norm(xp, g_ffn[l]), w_router_group[l], b_router_group[l],
                           w_router_expert[l], b_router_expert[l], w_exp_gate[l], w_exp_up[l], w_exp_down[l])
        k_prompt.append(k)
        v_prompt.append(v)
        n_seq, n_new = xs.shape[:2]
        q, k, v, u, vg, ga, gb = split_in(rms_norm(xs, g_mix[l]), w_in[l])
        q = q.reshape(n_seq, n_new, A_HEADS, HEAD_DIM)
        k = k.reshape(n_seq, n_new, A_HEADS, HEAD_DIM)
        v = v.reshape(n_seq, n_new, A_HEADS, HEAD_DIM)
        att = moba_sample(q, k, v, cache_k, cache_v, l, page_table, page_sums[l], slopes)
        sgu_out, v_rows = spatial_gating(u, vg, g_sgu[l], w_sgu[l], b_sgu[l])
        xs = xs + merge_out(att, sgu_out, ga, gb, w_proj_a[l], w_proj_b[l], w_out[l])
        xs = xs + hier_moe(rms_norm(xs, g_ffn[l]), w_router_group[l], b_router_group[l],
                           w_router_expert[l], b_router_expert[l], w_exp_gate[l], w_exp_up[l], w_exp_down[l])
        k_sample.append(k)
        v_sample.append(v)
        sgu_sample.append(v_rows)
    y_prompt = rms_norm(xp, g_final)
    y_sample = rms_norm(xs, g_final)
    new_k_prompt = jnp.stack(k_prompt)
    new_v_prompt = jnp.stack(v_prompt)
    new_k_sample = jnp.stack(k_sample)
    new_v_sample = jnp.stack(v_sample)
    new_sgu_v_sample = jnp.stack(sgu_sample)
    return (y_prompt, y_sample, new_k_prompt, new_v_prompt, new_k_sample, new_v_sample, new_sgu_v_sample)
```

```python
import functools

import jax
import jax.numpy as jnp
from jax import lax
from jax.experimental import pallas as pl
from jax.experimental.pallas import tpu as pltpu

NORM_EPS = 1e-6
NEG_INF = -1e30
HEAD_DIM = 128
MOBA_BLOCK = 256
MOBA_TOPK = 3
PAGE_SIZE = 128
PAGES_PER_BLOCK = MOBA_BLOCK // PAGE_SIZE
SGU_CHUNK = 128
SGU_GROUP_DIM = 128
N_EXP_GROUPS = 4
EXP_PER_GROUP = 4
N_EXPERTS = N_EXP_GROUPS * EXP_PER_GROUP
LANES = 128
ROUTER_EXPERT_LANE0 = N_EXP_GROUPS
VMEM_LIMIT_BYTES = 56 * 1024 * 1024
PAGES_PER_STEP = 16

F32 = jnp.float32
BF16 = jnp.bfloat16


def _params(*semantics):
    return pltpu.CompilerParams(dimension_semantics=semantics, vmem_limit_bytes=VMEM_LIMIT_BYTES)


def _rms_rows(x, g):
    inv = lax.rsqrt(jnp.mean(x * x, axis=-1, keepdims=True) + NORM_EPS)
    return (x * inv) * g


def _sigmoid(x):
    return 1.0 / (1.0 + jnp.exp(-x))


def _norm_matmul_kernel(x_ref, g_ref, w_ref, o_ref, xn_ref, *, row_chunk):
    @pl.when(pl.program_id(1) == 0)
    def _():
        def body(r, carry):
            rows = pl.ds(pl.multiple_of(r * row_chunk, row_chunk), row_chunk)
            xn_ref[rows, :] = _rms_rows(x_ref[rows, :], g_ref[...]).astype(xn_ref.dtype)
            return carry
        lax.fori_loop(0, x_ref.shape[0] // row_chunk, body, 0)

    o_ref[...] = jnp.dot(xn_ref[...], w_ref[...], preferred_element_type=F32).astype(o_ref.dtype)


def norm_matmul(x, g, w, out_dtype, tm, tn):
    t, d = x.shape
    n = w.shape[1]
    tm = min(tm, t)
    tn = min(tn, n)
    row_chunk = min(tm, 128)
    return pl.pallas_call(
        functools.partial(_norm_matmul_kernel, row_chunk=row_chunk),
        out_shape=jax.ShapeDtypeStruct((t, n), out_dtype),
        grid=(t // tm, n // tn),
        in_specs=[
            pl.BlockSpec((tm, d), lambda i, j: (i, 0)),
            pl.BlockSpec((1, d), lambda i, j: (0, 0)),
            pl.BlockSpec((d, tn), lambda i, j: (0, j)),
        ],
        out_specs=pl.BlockSpec((tm, tn), lambda i, j: (i, j)),
        scratch_shapes=[pltpu.VMEM((tm, d), BF16)],
        compiler_params=_params("parallel", "arbitrary"),
        name="norm_matmul",
    )(x, g.reshape(1, d), w)


def _moba_prompt_kernel(q_ref, k_ref, v_ref, o_ref, kaug_ref, vb_ref, kmean_ref, bias_ref, biasd_ref,
                        gate_ref, *, n_blk, n_heads):
    blk = MOBA_BLOCK
    h = pl.program_id(1)
    qi = pl.program_id(2)
    hd = q_ref.shape[1]
    scale = hd ** -0.5
    slope = jnp.exp2(jnp.full((1, 1), -8.0 / n_heads, F32) * (h + 1).astype(F32))

    @pl.when(qi == 0)
    def _():
        lane = lax.broadcasted_iota(jnp.int32, (blk, hd), 1)
        kmean_ref[...] = jnp.zeros_like(kmean_ref)
        for n in range(n_blk):
            rows = slice(n * blk, (n + 1) * blk)
            kb = k_ref[rows, :]
            kmean_ref[n:n + 1, :] = jnp.sum(kb, axis=0, keepdims=True) * (1.0 / blk)
            kaug_ref[rows, 0:hd] = kb.astype(BF16)
            kaug_ref[rows, hd:2 * hd] = jnp.where(lane == n, 1.0, 0.0).astype(BF16)
            vb_ref[rows, :] = v_ref[rows, :].astype(BF16)
        r = lax.broadcasted_iota(jnp.int32, (blk, blk), 0)
        c = lax.broadcasted_iota(jnp.int32, (blk, blk), 1)
        alibi = -slope * (r - c).astype(F32)
        bias_ref[...] = alibi
        biasd_ref[...] = jnp.where(c <= r, alibi, NEG_INF)

    q = q_ref[...]
    qb = q.astype(BF16)

    gate_ref[...] = lax.dot_general(kmean_ref[...], q, (((1,), (1,)), ((), ())),
                                    preferred_element_type=F32, precision=lax.Precision.HIGHEST)
    g16 = gate_ref[0:16, :]
    row = lax.broadcasted_iota(jnp.int32, g16.shape, 0)
    rank = jnp.zeros(g16.shape, jnp.int32)
    for j in range(n_blk):
        gj = gate_ref[j:j + 1, :]
        ahead = (gj > g16) | ((gj == g16) & (j < row))
        rank = rank + jnp.where(ahead & (j < qi), 1, 0)
    chosen = (row < qi) & (rank < MOBA_TOPK)
    gate_ref[...] = jnp.zeros_like(gate_ref)
    gate_ref[0:16, :] = jnp.where(chosen, 0.0, NEG_INF)
    q_aug = jnp.concatenate([qb, gate_ref[...].T.astype(BF16)], axis=1)

    own = pl.ds(pl.multiple_of(qi * blk, blk), blk)
    s = lax.dot_general(qb, kaug_ref[own, 0:hd], (((1,), (1,)), ((), ())),
                        preferred_element_type=F32) * scale + biasd_ref[...]
    m0 = jnp.max(s, axis=1, keepdims=True)
    p = jnp.exp(s - m0)
    l0 = jnp.sum(p, axis=1, keepdims=True)
    acc0 = jnp.dot(p.astype(BF16), vb_ref[own, :], preferred_element_type=F32)

    def body(j, carry):
        m, l, acc = carry
        rows = pl.ds(pl.multiple_of(j * blk, blk), blk)
        s = lax.dot_general(q_aug, kaug_ref[rows, :], (((1,), (1,)), ((), ())),
                            preferred_element_type=F32) * scale + bias_ref[...]
        off = -slope * (blk * (qi - j)).astype(F32)
        m_new = jnp.maximum(m, jnp.max(s, axis=1, keepdims=True) + off)
        alpha = jnp.exp(m - m_new)
        p = jnp.exp(s - (m_new - off))
        l = alpha * l + jnp.sum(p, axis=1, keepdims=True)
        acc = alpha * acc + jnp.dot(p.astype(BF16), vb_ref[rows, :], preferred_element_type=F32)
        return m_new, l, acc

    _, l, acc = lax.fori_loop(0, qi, body, (m0, l0, acc0))
    o_ref[...] = (acc / l).astype(o_ref.dtype)


def moba_prompt(q, k, v, b_sz, s_len, out_dtype):
    t, width = q.shape
    hd = HEAD_DIM
    n_heads = width // hd
    blk = MOBA_BLOCK
    assert s_len % blk == 0 and s_len // blk <= 16
    n_blk = s_len // blk
    return pl.pallas_call(
        functools.partial(_moba_prompt_kernel, n_blk=n_blk, n_heads=n_heads),
        out_shape=jax.ShapeDtypeStruct((t, width), out_dtype),
        grid=(b_sz, n_heads, n_blk),
        in_specs=[
            pl.BlockSpec((blk, hd), lambda b, h, i: (b * n_blk + i, h)),
            pl.BlockSpec((s_len, hd), lambda b, h, i: (b, h)),
            pl.BlockSpec((s_len, hd), lambda b, h, i: (b, h)),
        ],
        out_specs=pl.BlockSpec((blk, hd), lambda b, h, i: (b * n_blk + i, h)),
        scratch_shapes=[
            pltpu.VMEM((s_len, 2 * hd), BF16),
            pltpu.VMEM((s_len, hd), BF16),
            pltpu.VMEM((LANES, hd), F32),
            pltpu.VMEM((blk, blk), F32),
            pltpu.VMEM((blk, blk), F32),
            pltpu.VMEM((LANES, blk), F32),
        ],
        compiler_params=_params("parallel", "parallel", "arbitrary"),
        name="moba_prompt",
    )(q, k, v)


def _page_sums_kernel(pt_ref, *refs):
    del pt_ref
    page_refs, o_ref = refs[:-1], refs[-1]
    sums = [jnp.sum(r[...], axis=0) for r in page_refs]
    for i in range(len(sums) // PAGES_PER_BLOCK):
        o_ref[i] = sums[2 * i] + sums[2 * i + 1]


def page_block_sums(cache_k, layer, page_table_flat, n_seq, n_pages):
    _, _, page, n_heads, hd = cache_k.shape
    pps = PAGES_PER_STEP
    assert n_pages % pps == 0
    n_blk = n_pages // PAGES_PER_BLOCK

    def page_spec(r):
        return pl.BlockSpec((None, None, page, n_heads, hd),
                            lambda b, g, pt: (layer, pt[b * n_pages + g * pps + r], 0, 0, 0))

    return pl.pallas_call(
        _page_sums_kernel,
        out_shape=jax.ShapeDtypeStruct((n_seq, n_blk, n_heads, hd), F32),
        grid_spec=pltpu.PrefetchScalarGridSpec(
            num_scalar_prefetch=1,
            grid=(n_seq, n_pages // pps),
            in_specs=[page_spec(r) for r in range(pps)],
            out_specs=pl.BlockSpec((None, pps // PAGES_PER_BLOCK, n_heads, hd),
                                   lambda b, g, pt: (b, g, 0, 0)),
        ),
        compiler_params=_params("parallel", "arbitrary"),
        name="page_block_sums",
    )(page_table_flat, *([cache_k] * pps))


def _block_choice_kernel(bs_ref, q_ref, o_ref):
    bs = bs_ref[...]
    gate = jnp.broadcast_to(jnp.sum(bs * q_ref[...], axis=2, keepdims=True), bs.shape)
    blk_id = lax.broadcasted_iota(jnp.int32, bs.shape, 0).astype(F32)
    o_ref[...] = jnp.zeros_like(o_ref)
    for s in range(MOBA_TOPK):
        mx = jnp.max(gate, axis=0, keepdims=True)
        idx = jnp.min(jnp.where(gate == mx, blk_id, 1e9), axis=0, keepdims=True)
        o_ref[s] = idx[0].astype(jnp.int32)
        gate = jnp.where(blk_id == idx, -jnp.inf, gate)


def block_choice(blk_sums, q_s):
    n_seq, n_blk, n_heads, hd = blk_sums.shape
    return pl.pallas_call(
        _block_choice_kernel,
        out_shape=jax.ShapeDtypeStruct((n_seq, 8, n_heads, hd), jnp.int32),
        grid=(n_seq,),
        in_specs=[
            pl.BlockSpec((None, n_blk, n_heads, hd), lambda b: (b, 0, 0, 0)),
            pl.BlockSpec((None, n_heads, hd), lambda b: (b, 0, 0)),
        ],
        out_specs=pl.BlockSpec((None, 8, n_heads, hd), lambda b: (b, 0, 0, 0)),
        compiler_params=_params("parallel"),
        name="block_choice",
    )(blk_sums, q_s.reshape(n_seq, n_heads, hd))


def _sample_attend_kernel(phys_ref, idx_ref, q_ref, kn_ref, vn_ref, *refs, n_heads, past_len):
    del phys_ref
    n_sel = MOBA_TOPK * PAGES_PER_BLOCK
    k_refs, v_refs, o_ref = refs[:n_sel], refs[n_sel:2 * n_sel], refs[-1]
    b = pl.program_id(0)
    h = pl.program_id(1)
    hd = HEAD_DIM
    scale = hd ** -0.5
    col = lax.broadcasted_iota(jnp.int32, (8, PAGE_SIZE), 1)

    def one_head(hh):
        slope = 2.0 ** (-8.0 * (hh + 1) / n_heads)
        q = q_ref[hh:hh + 1, :]
        q8 = jnp.broadcast_to(q, (8, hd))
        scores = []
        for s in range(n_sel):
            blk = idx_ref[(b * n_heads + h) * MOBA_TOPK + s // PAGES_PER_BLOCK]
            pos0 = blk * MOBA_BLOCK + (s % PAGES_PER_BLOCK) * PAGE_SIZE
            dist = (past_len - pos0 - col).astype(F32)
            qk = lax.dot_general(q8, k_refs[s][:, hh, :], (((1,), (1,)), ((), ())),
                                 preferred_element_type=F32, precision=lax.Precision.HIGHEST)
            scores.append(qk * scale - slope * dist)
        s_self = jnp.sum(q * kn_ref[hh:hh + 1, :], axis=1, keepdims=True) * scale

        m = s_self
        for sc in scores:
            m = jnp.maximum(m, jnp.max(sc[0:1, :], axis=1, keepdims=True))
        p_self = jnp.exp(s_self - m)
        l = p_self
        acc = p_self * vn_ref[hh:hh + 1, :]
        for s in range(n_sel):
            p = jnp.exp(scores[s] - m)
            l = l + jnp.sum(p[0:1, :], axis=1, keepdims=True)
            pv = jnp.dot(p, v_refs[s][:, hh, :], preferred_element_type=F32,
                         precision=lax.Precision.HIGHEST)
            acc = acc + pv[0:1, :]
        o_ref[hh:hh + 1, :] = acc / l

    for hh in range(n_heads):
        pl.when(h == hh)(functools.partial(one_head, hh))


def sample_attend(q_s, k_s, v_s, cache_k, cache_v, layer, phys, idx_flat, past_len):
    n_seq, width = q_s.shape
    hd = HEAD_DIM
    n_heads = width // hd
    n_sel = MOBA_TOPK * PAGES_PER_BLOCK

    def page_spec(s):
        return pl.BlockSpec((None, None, PAGE_SIZE, n_heads, hd),
                            lambda b, h, ph, ix: (layer, ph[(b * n_heads + h) * n_sel + s], 0, 0, 0))

    row_spec = pl.BlockSpec((None, n_heads, hd), lambda b, h, ph, ix: (b, 0, 0))
    to3 = lambda a: a.reshape(n_seq, n_heads, hd)
    out = pl.pallas_call(
        functools.partial(_sample_attend_kernel, n_heads=n_heads, past_len=past_len),
        out_shape=jax.ShapeDtypeStruct((n_seq, n_heads, hd), F32),
        grid_spec=pltpu.PrefetchScalarGridSpec(
            num_scalar_prefetch=2,
            grid=(n_seq, n_heads),
            in_specs=[row_spec, row_spec, row_spec] + [page_spec(s) for s in range(n_sel)] * 2,
            out_specs=row_spec,
        ),
        compiler_params=_params("parallel", "arbitrary"),
        name="sample_attend",
    )(phys, idx_flat, to3(q_s), to3(k_s), to3(v_s), *([cache_k] * n_sel), *([cache_v] * n_sel))
    return out.reshape(n_seq, width)


def _sgu_merge_kernel(x_ref, att_ref, u_ref, vg_ref, ga_ref, gb_ref, gs_ref, ws_ref, bs_ref,
                      pa_ref, pb_ref, wo_ref, *refs, chunked):
    if chunked:
        o_ref, m_ref, acc_ref = refs
    else:
        o_ref, vn_out_ref, m_ref, acc_ref = refs
    j = pl.program_id(1)
    tm = x_ref.shape[0]

    @pl.when(j == 0)
    def _():
        vn = _rms_rows(vg_ref[...].astype(F32), gs_ref[...])
        if chunked:
            ch = SGU_CHUNK
            gd = SGU_GROUP_DIM
            r = lax.broadcasted_iota(jnp.int32, (ch, ch), 0)
            c = lax.broadcasted_iota(jnp.int32, (ch, ch), 1)
            vnb = vn.astype(BF16)
            for g in range(ws_ref.shape[0]):
                w = jnp.where(c <= r, ws_ref[g], 0.0).astype(BF16)
                bias = bs_ref[g]
                for cc in range(tm // ch):
                    rows = slice(cc * ch, (cc + 1) * ch)
                    cols = slice(g * gd, (g + 1) * gd)
                    z = jnp.dot(w, vnb[rows, cols], preferred_element_type=F32) + bias
                    m_ref[rows, cols] = (u_ref[rows, cols].astype(F32) * z).astype(m_ref.dtype)
        else:
            vn_out_ref[...] = vn
            z = vn * ws_ref[...] + bs_ref[...]
            m_ref[...] = (u_ref[...].astype(F32) * z).astype(m_ref.dtype)

    pa = jnp.dot(att_ref[...].astype(BF16), pa_ref[...], preferred_element_type=F32)
    pb = jnp.dot(m_ref[...], pb_ref[...], preferred_element_type=F32)
    mixed = _sigmoid(ga_ref[...].astype(F32)) * pa + _sigmoid(gb_ref[...].astype(F32)) * pb
    part = jnp.dot(mixed.astype(BF16), wo_ref[...], preferred_element_type=F32)

    @pl.when(j == 0)
    def _():
        acc_ref[...] = part

    @pl.when(j != 0)
    def _():
        acc_ref[...] += part

    @pl.when(j == pl.num_programs(1) - 1)
    def _():
        o_ref[...] = x_ref[...] + acc_ref[...]


def sgu_merge(x, att, rest, g_sgu, w_sgu, b_sgu, w_pa, w_pb, w_o, chunked, tm, tn):
    t, d = x.shape
    a_w = att.shape[1]
    b_w = g_sgu.shape[0]
    tm = min(tm, t)
    assert d % tn == 0 and b_w % tn == 0 and a_w == b_w and tn == b_w
    nj = d // tn
    n_groups = w_sgu.shape[0]
    if chunked:
        assert tm % SGU_CHUNK == 0
        ws = w_sgu
        bs = jnp.broadcast_to(b_sgu[:, :, None], (n_groups, SGU_CHUNK, SGU_GROUP_DIM))
        ws_spec = pl.BlockSpec((n_groups, SGU_CHUNK, SGU_CHUNK), lambda i, j: (0, 0, 0))
        bs_spec = pl.BlockSpec((n_groups, SGU_CHUNK, SGU_GROUP_DIM), lambda i, j: (0, 0, 0))
    else:
        ws = jnp.repeat(w_sgu[:, 0, 0], SGU_GROUP_DIM).reshape(1, b_w)
        bs = jnp.repeat(b_sgu[:, 0], SGU_GROUP_DIM).reshape(1, b_w)
        ws_spec = bs_spec = pl.BlockSpec((1, b_w), lambda i, j: (0, 0))
    u_blk = b_w // tn
    in_specs = [
        pl.BlockSpec((tm, d), lambda i, j: (i, 0)),
        pl.BlockSpec((tm, a_w), lambda i, j: (i, 0)),
        pl.BlockSpec((tm, b_w), lambda i, j: (i, 0)),
        pl.BlockSpec((tm, b_w), lambda i, j: (i, 1)),
        pl.BlockSpec((tm, tn), lambda i, j: (i, 2 * u_blk + j)),
        pl.BlockSpec((tm, tn), lambda i, j: (i, 2 * u_blk + nj + j)),
        pl.BlockSpec((1, b_w), lambda i, j: (0, 0)),
        ws_spec, bs_spec,
        pl.BlockSpec((a_w, tn), lambda i, j: (0, j)),
        pl.BlockSpec((b_w, tn), lambda i, j: (0, j)),
        pl.BlockSpec((tn, d), lambda i, j: (j, 0)),
    ]
    x_spec = pl.BlockSpec((tm, d), lambda i, j: (i, 0))
    if chunked:
        out_shape = jax.ShapeDtypeStruct((t, d), F32)
        out_specs = x_spec
    else:
        out_shape = (jax.ShapeDtypeStruct((t, d), F32), jax.ShapeDtypeStruct((t, b_w), F32))
        out_specs = (x_spec, pl.BlockSpec((tm, b_w), lambda i, j: (i, 0)))
    return pl.pallas_call(
        functools.partial(_sgu_merge_kernel, chunked=chunked),
        out_shape=out_shape,
        grid=(t // tm, nj),
        in_specs=in_specs,
        out_specs=out_specs,
        scratch_shapes=[pltpu.VMEM((tm, b_w), BF16), pltpu.VMEM((tm, d), F32)],
        compiler_params=_params("parallel", "arbitrary"),
        name="sgu_merge",
    )(x, att, rest, rest, rest, rest, g_sgu.reshape(1, b_w), ws, bs, w_pa, w_pb, w_o)


def _moe_route_kernel(x_ref, g_ref, wr_ref, br_ref, xn_ref, comb_ref):
    xn = _rms_rows(x_ref[...], g_ref[...])
    xn_ref[...] = xn.astype(xn_ref.dtype)
    logits = jnp.dot(xn, wr_ref[...], preferred_element_type=F32,
                     precision=lax.Precision.HIGHEST) + br_ref[...]
    lane = lax.broadcasted_iota(jnp.int32, logits.shape, 1)
    lanef = lane.astype(F32)
    e0 = ROUTER_EXPERT_LANE0
    is_grp = lane < e0
    gl = jnp.where(is_grp, logits, -jnp.inf)
    gmax = jnp.max(gl, axis=1, keepdims=True)
    grp = jnp.min(jnp.where(gl == gmax, lanef, 1e9), axis=1, keepdims=True)
    p_grp = 1.0 / jnp.sum(jnp.where(is_grp, jnp.exp(logits - gmax), 0.0), axis=1, keepdims=True)
    lane_grp = jnp.right_shift(lane - e0, 2).astype(F32)
    in_grp = (lane >= e0) & (lane < e0 + N_EXPERTS) & (lane_grp == grp)
    el = jnp.where(in_grp, logits, -jnp.inf)
    t1 = jnp.max(el, axis=1, keepdims=True)
    i1 = jnp.min(jnp.where(el == t1, lanef, 1e9), axis=1, keepdims=True)
    el2 = jnp.where(lanef == i1, -jnp.inf, el)
    t2 = jnp.max(el2, axis=1, keepdims=True)
    i2 = jnp.min(jnp.where(el2 == t2, lanef, 1e9), axis=1, keepdims=True)
    e2 = jnp.exp(t2 - t1)
    w1 = (1.0 / (1.0 + e2)) * p_grp
    w2 = (e2 / (1.0 + e2)) * p_grp
    comb_ref[...] = jnp.where(lanef == i1, w1, jnp.where(lanef == i2, w2, 0.0))


def moe_route(x, g_ffn, w_rg, b_rg, w_re, b_re, tm):
    t, d = x.shape
    tm = min(tm, t)
    assert EXP_PER_GROUP == 4 and w_rg.shape[1] == N_EXP_GROUPS and w_re.shape[1] == N_EXPERTS
    pad = LANES - N_EXP_GROUPS - N_EXPERTS
    wr = jnp.concatenate([w_rg, w_re, jnp.zeros((d, pad), F32)], axis=1)
    br = jnp.concatenate([b_rg, b_re, jnp.zeros((pad,), F32)]).reshape(1, LANES)
    return pl.pallas_call(
        _moe_route_kernel,
        out_shape=(jax.ShapeDtypeStruct((t, d), BF16), jax.ShapeDtypeStruct((t, LANES), F32)),
        grid=(t // tm,),
        in_specs=[
            pl.BlockSpec((tm, d), lambda i: (i, 0)),
            pl.BlockSpec((1, d), lambda i: (0, 0)),
            pl.BlockSpec((d, LANES), lambda i: (0, 0)),
            pl.BlockSpec((1, LANES), lambda i: (0, 0)),
        ],
        out_specs=(pl.BlockSpec((tm, d), lambda i: (i, 0)), pl.BlockSpec((tm, LANES), lambda i: (i, 0))),
        compiler_params=_params("parallel"),
        name="moe_route",
    )(x, g_ffn.reshape(1, d), wr, br)


def _moe_experts_kernel(xn_ref, comb_ref, x_ref, wg_ref, wu_ref, wd_ref, gf_ref, o_ref, acc_ref):
    e = pl.program_id(1)
    lane = lax.broadcasted_iota(jnp.int32, comb_ref.shape, 1)
    c = jnp.sum(jnp.where(lane == e + ROUTER_EXPERT_LANE0, comb_ref[...], 0.0), axis=1, keepdims=True)
    xn = xn_ref[...]
    g = jnp.dot(xn, wg_ref[...], preferred_element_type=F32)
    u = jnp.dot(xn, wu_ref[...], preferred_element_type=F32)
    h = (g * _sigmoid(g)) * u
    part = jnp.dot((h * c).astype(BF16), wd_ref[...], preferred_element_type=F32)

    @pl.when(e == 0)
    def _():
        acc_ref[...] = part

    @pl.when(e != 0)
    def _():
        acc_ref[...] += part

    @pl.when(e == pl.num_programs(1) - 1)
    def _():
        o_ref[...] = _rms_rows(x_ref[...] + acc_ref[...], gf_ref[...])


def moe_experts(xn, comb, x, w_gate, w_up, w_down, g_final, tm):
    t, d = x.shape
    n_exp, _, f = w_gate.shape
    tm = min(tm, t)
    return pl.pallas_call(
        _moe_experts_kernel,
        out_shape=jax.ShapeDtypeStruct((t, d), F32),
        grid=(t // tm, n_exp),
        in_specs=[
            pl.BlockSpec((tm, d), lambda i, e: (i, 0)),
            pl.BlockSpec((tm, LANES), lambda i, e: (i, 0)),
            pl.BlockSpec((tm, d), lambda i, e: (i, 0)),
            pl.BlockSpec((None, d, f), lambda i, e: (e, 0, 0)),
            pl.BlockSpec((None, d, f), lambda i, e: (e, 0, 0)),
            pl.BlockSpec((None, f, d), lambda i, e: (e, 0, 0)),
            pl.BlockSpec((1, d), lambda i, e: (0, 0)),
        ],
        out_specs=pl.BlockSpec((tm, d), lambda i, e: (i, 0)),
        scratch_shapes=[pltpu.VMEM((tm, d), F32)],
        compiler_params=_params("parallel", "arbitrary"),
        name="moe_experts",
    )(xn, comb, x, w_gate, w_up, w_down, g_final.reshape(1, d))


def kernel(x_prompt, x_sample, cache_k, cache_v, page_table, g_mix, w_in, g_sgu, w_sgu, b_sgu, w_proj_a, w_proj_b, w_out, g_ffn, w_router_group, b_router_group, w_router_expert, b_router_expert, w_exp_gate, w_exp_up, w_exp_down, g_final):
    depth = g_mix.shape[0]
    assert depth == 1
    layer = 0
    b_sz, s_len, d = x_prompt.shape
    n_seq, n_new, _ = x_sample.shape
    assert n_new == 1
    _, n_pool, page, n_heads, hd = cache_k.shape
    assert page == PAGE_SIZE and hd == HEAD_DIM
    a_w = n_heads * hd
    b_w = g_sgu.shape[1]
    n_pages = page_table.shape[1]
    past_len = n_pages * PAGE_SIZE
    assert past_len % MOBA_BLOCK == 0 and n_pages // PAGES_PER_BLOCK >= MOBA_TOPK

    w_in_l = w_in[layer]
    w_q = w_in_l[:, 0:a_w].astype(BF16)
    w_k = w_in_l[:, a_w:2 * a_w].astype(BF16)
    w_v = w_in_l[:, 2 * a_w:3 * a_w].astype(BF16)
    w_rest = w_in_l[:, 3 * a_w:].astype(BF16)
    w_pa = w_proj_a[layer].astype(BF16)
    w_pb = w_proj_b[layer].astype(BF16)
    w_o = w_out[layer].astype(BF16)
    w_eg = w_exp_gate[layer].astype(BF16)
    w_eu = w_exp_up[layer].astype(BF16)
    w_ed = w_exp_down[layer].astype(BF16)

    def dense_tail(x2, att, rest, chunked):
        merged = sgu_merge(x2, att, rest, g_sgu[layer], w_sgu[layer], b_sgu[layer], w_pa, w_pb, w_o,
                           chunked=chunked, tm=512, tn=1024)
        x1, vn = (merged, None) if chunked else merged
        xn, comb = moe_route(x1, g_ffn[layer], w_router_group[layer], b_router_group[layer],
                             w_router_expert[layer], b_router_expert[layer], tm=256)
        y = moe_experts(xn, comb, x1, w_eg, w_eu, w_ed, g_final, tm=512)
        return y, vn

    xp = x_prompt.reshape(b_sz * s_len, d)
    g_in = g_mix[layer]
    q_p = norm_matmul(xp, g_in, w_q, F32, tm=1024, tn=1024)
    k_p = norm_matmul(xp, g_in, w_k, F32, tm=1024, tn=1024)
    v_p = norm_matmul(xp, g_in, w_v, F32, tm=1024, tn=1024)
    rest_p = norm_matmul(xp, g_in, w_rest, BF16, tm=1024, tn=1024)
    att_p = moba_prompt(q_p, k_p, v_p, b_sz, s_len, BF16)
    y_p, _ = dense_tail(xp, att_p, rest_p, chunked=True)

    xs = x_sample.reshape(n_seq, d)
    q_s = norm_matmul(xs, g_in, w_q, F32, tm=n_seq, tn=1024)
    k_s = norm_matmul(xs, g_in, w_k, F32, tm=n_seq, tn=1024)
    v_s = norm_matmul(xs, g_in, w_v, F32, tm=n_seq, tn=1024)
    rest_s = norm_matmul(xs, g_in, w_rest, F32, tm=n_seq, tn=1024)
    pt_flat = page_table.reshape(-1)
    blk_sums = page_block_sums(cache_k, layer, pt_flat, n_seq, n_pages)
    picks = block_choice(blk_sums, q_s)
    idx = jnp.transpose(picks[:, :MOBA_TOPK, :, 0], (0, 2, 1))
    pages = idx[..., None] * PAGES_PER_BLOCK + jnp.arange(PAGES_PER_BLOCK, dtype=jnp.int32)
    phys = jnp.take_along_axis(page_table[:, None, :], pages.reshape(n_seq, n_heads, -1), axis=2)
    att_s = sample_attend(q_s, k_s, v_s, cache_k, cache_v, layer, phys.reshape(-1), idx.reshape(-1),
                          past_len)
    y_s, vn_s = dense_tail(xs, att_s, rest_s, chunked=False)

    return (
        y_p.reshape(b_sz, s_len, d),
        y_s.reshape(n_seq, n_new, d),
        k_p.reshape(depth, b_sz, s_len, n_heads, hd),
        v_p.reshape(depth, b_sz, s_len, n_heads, hd),
        k_s.reshape(depth, n_seq, n_new, n_heads, hd),
        v_s.reshape(depth, n_seq, n_new, n_heads, hd),
        vn_s.reshape(depth, n_seq, n_new, b_w),
    )
```

```python
import functools

import jax
import jax.numpy as jnp
from jax import lax
from jax.experimental import pallas as pl
from jax.experimental.pallas import tpu as pltpu

NORM_EPS = 1e-6
NEG_INF = -1e30
HEAD_DIM = 128
MOBA_BLOCK = 256
MOBA_TOPK = 3
PAGE_SIZE = 128
PAGES_PER_BLOCK = MOBA_BLOCK // PAGE_SIZE
SGU_CHUNK = 128
SGU_GROUP_DIM = 128
N_EXP_GROUPS = 4
EXP_PER_GROUP = 4
N_EXPERTS = N_EXP_GROUPS * EXP_PER_GROUP
TOP_K_IN_GROUP = 2
MOE_ROW_TILE = 256
MOE_SPARSE_MIN_TOKENS = 2048
LANES = 128
ROUTER_EXPERT_LANE0 = N_EXP_GROUPS
VMEM_LIMIT_BYTES = 56 * 1024 * 1024
PAGES_PER_STEP = 16
MOBA_HEADS_PER_STEP = 8
LOG2E = 1.4426950408889634

F32 = jnp.float32
BF16 = jnp.bfloat16


def _params(*semantics):
    return pltpu.CompilerParams(dimension_semantics=semantics, vmem_limit_bytes=VMEM_LIMIT_BYTES)


def _rms_rows(x, g):
    inv = lax.rsqrt(jnp.mean(x * x, axis=-1, keepdims=True) + NORM_EPS)
    return (x * inv) * g


def _sigmoid(x):
    return 1.0 / (1.0 + jnp.exp(-x))


def _norm_matmul_kernel(x_ref, g_ref, w_ref, *refs, row_chunk):
    out_refs, xn_ref = refs[:-1], refs[-1]

    @pl.when(pl.program_id(1) == 0)
    def _():
        def body(r, carry):
            rows = pl.ds(pl.multiple_of(r * row_chunk, row_chunk), row_chunk)
            xn_ref[rows, :] = _rms_rows(x_ref[rows, :], g_ref[...]).astype(xn_ref.dtype)
            return carry
        lax.fori_loop(0, x_ref.shape[0] // row_chunk, body, 0)

    res = jnp.dot(xn_ref[...], w_ref[...], preferred_element_type=F32)
    for o_ref in out_refs:
        o_ref[...] = res.astype(o_ref.dtype)


def norm_matmul(x, g, w, out_dtypes, tm, tn):
    t, d = x.shape
    n = w.shape[1]
    tm = min(tm, t)
    tn = min(tn, n)
    row_chunk = min(tm, 128)
    out_spec = pl.BlockSpec((tm, tn), lambda i, j: (i, j))
    return pl.pallas_call(
        functools.partial(_norm_matmul_kernel, row_chunk=row_chunk),
        out_shape=tuple(jax.ShapeDtypeStruct((t, n), dt) for dt in out_dtypes),
        grid=(t // tm, n // tn),
        in_specs=[
            pl.BlockSpec((tm, d), lambda i, j: (i, 0)),
            pl.BlockSpec((1, d), lambda i, j: (0, 0)),
            pl.BlockSpec((d, tn), lambda i, j: (0, j)),
        ],
        out_specs=tuple(out_spec for _ in out_dtypes),
        scratch_shapes=[pltpu.VMEM((tm, d), BF16)],
        compiler_params=_params("parallel", "arbitrary"),
        name="norm_matmul",
    )(x, g.reshape(1, d), w)


def _block_means_kernel(k_ref, o_ref):
    blk = MOBA_BLOCK
    for n in range(o_ref.shape[0]):
        o_ref[n:n + 1, :] = jnp.sum(k_ref[n * blk:(n + 1) * blk, :], axis=0, keepdims=True) * (1.0 / blk)


def block_means(k):
    t, width = k.shape
    per_step = 8
    rows = per_step * MOBA_BLOCK
    assert t % rows == 0
    return pl.pallas_call(
        _block_means_kernel,
        out_shape=jax.ShapeDtypeStruct((t // MOBA_BLOCK, width), F32),
        grid=(t // rows,),
        in_specs=[pl.BlockSpec((rows, width), lambda i: (i, 0))],
        out_specs=pl.BlockSpec((per_step, width), lambda i: (i, 0)),
        compiler_params=_params("parallel"),
        name="block_means",
    )(k)


def _moba_prompt_kernel(q_ref, k_ref, v_ref, kmean_ref, o_ref, e_ref, bias_ref, biasd_ref, gate_ref,
                        *, n_blk, n_heads, hp):
    blk = MOBA_BLOCK
    hd = HEAD_DIM
    grp = pl.program_id(1)
    qi = pl.program_id(2)
    c2 = hd ** -0.5 * LOG2E
    nt = (((1,), (1,)), ((), ()))
    slopes = [jnp.exp2(jnp.full((1, 1), -8.0 / n_heads, F32) * (grp * hp + t + 1).astype(F32)) * LOG2E
              for t in range(hp)]

    @pl.when(qi == 0)
    def _():
        lane = lax.broadcasted_iota(jnp.int32, (blk, hd), 1)
        for n in range(n_blk):
            e_ref[n * blk:(n + 1) * blk, :] = jnp.where(lane == n, 1.0, 0.0).astype(BF16)
        r = lax.broadcasted_iota(jnp.int32, (blk, blk), 0)
        c = lax.broadcasted_iota(jnp.int32, (blk, blk), 1)
        for t in range(hp):
            alibi = -slopes[t] * (r - c).astype(F32)
            bias_ref[t] = alibi
            biasd_ref[t] = jnp.where(c <= r, alibi, NEG_INF)

    own = pl.ds(pl.multiple_of(qi * blk, blk), blk)
    q_augs, carry0 = [], []
    for t in range(hp):
        cols = slice(t * hd, (t + 1) * hd)
        q = q_ref[:, cols]
        qb = q.astype(BF16)
        gate_ref[t, 0:n_blk, :] = lax.dot_general(kmean_ref[:, cols], q, nt, preferred_element_type=F32,
                                                  precision=lax.Precision.HIGHEST)
        g_all = gate_ref[t, 0:n_blk, :]
        row = lax.broadcasted_iota(jnp.int32, g_all.shape, 0)
        rank = jnp.zeros(g_all.shape, jnp.int32)
        for j in range(n_blk):
            gj = gate_ref[t, j:j + 1, :]
            ahead = (gj > g_all) | ((gj == g_all) & (j < row))
            rank = rank + jnp.where(ahead & (j < qi), 1, 0)
        chosen = (row < qi) & (rank < MOBA_TOPK)
        gate_ref[t] = jnp.zeros(gate_ref.shape[1:], F32)
        gate_ref[t, 0:n_blk, :] = jnp.where(chosen, 0.0, NEG_INF)
        q_augs.append(jnp.concatenate([qb, gate_ref[t].T.astype(BF16)], axis=1))

        s = lax.dot_general(qb, k_ref[own, cols], nt, preferred_element_type=F32) * c2 + biasd_ref[t]
        m0 = jnp.max(s, axis=1, keepdims=True)
        p = jnp.exp2(s - m0)
        l0 = jnp.sum(p, axis=1, keepdims=True)
        acc0 = jnp.dot(p.astype(BF16), v_ref[own, cols], preferred_element_type=F32)
        carry0 += [m0, l0, acc0]

    def body(j, carry):
        rows = pl.ds(pl.multiple_of(j * blk, blk), blk)
        dist = (blk * (qi - j)).astype(F32)
        e_j = e_ref[rows, :]
        out = []
        for t in range(hp):
            m, l, acc = carry[3 * t:3 * t + 3]
            cols = slice(t * hd, (t + 1) * hd)
            k_aug = jnp.concatenate([k_ref[rows, cols], e_j], axis=1)
            s = lax.dot_general(q_augs[t], k_aug, nt, preferred_element_type=F32) * c2 + bias_ref[t]
            off = -slopes[t] * dist
            m_new = jnp.maximum(m, jnp.max(s, axis=1, keepdims=True) + off)
            alpha = jnp.exp2(m - m_new)
            p = jnp.exp2(s - (m_new - off))
            l = alpha * l + jnp.sum(p, axis=1, keepdims=True)
            acc = alpha * acc + jnp.dot(p.astype(BF16), v_ref[rows, cols], preferred_element_type=F32)
            out += [m_new, l, acc]
        return tuple(out)

    final = lax.fori_loop(0, qi, body, tuple(carry0))
    for t in range(hp):
        _, l, acc = final[3 * t:3 * t + 3]
        o_ref[:, t * hd:(t + 1) * hd] = (acc / l).astype(o_ref.dtype)


def moba_prompt(q, k_bf, v_bf, kmeans, b_sz, s_len, out_dtype):
    t, width = q.shape
    hd = HEAD_DIM
    n_heads = width // hd
    hp = MOBA_HEADS_PER_STEP
    blk = MOBA_BLOCK
    n_blk = s_len // blk
    assert s_len % blk == 0 and n_blk % 8 == 0 and n_blk <= LANES and n_heads % hp == 0
    return pl.pallas_call(
        functools.partial(_moba_prompt_kernel, n_blk=n_blk, n_heads=n_heads, hp=hp),
        out_shape=jax.ShapeDtypeStruct((t, width), out_dtype),
        grid=(b_sz, n_heads // hp, n_blk),
        in_specs=[
            pl.BlockSpec((blk, hp * hd), lambda b, g, i: (b * n_blk + i, g)),
            pl.BlockSpec((s_len, hp * hd), lambda b, g, i: (b, g)),
            pl.BlockSpec((s_len, hp * hd), lambda b, g, i: (b, g)),
            pl.BlockSpec((n_blk, hp * hd), lambda b, g, i: (b, g)),
        ],
        out_specs=pl.BlockSpec((blk, hp * hd), lambda b, g, i: (b * n_blk + i, g)),
        scratch_shapes=[
            pltpu.VMEM((s_len, hd), BF16),
            pltpu.VMEM((hp, blk, blk), F32),
            pltpu.VMEM((hp, blk, blk), F32),
            pltpu.VMEM((hp, LANES, blk), F32),
        ],
        compiler_params=_params("parallel", "parallel", "arbitrary"),
        name="moba_prompt",
    )(q, k_bf, v_bf, kmeans)


def _page_sums_kernel(pt_ref, *refs):
    del pt_ref
    page_refs, o_ref = refs[:-1], refs[-1]
    sums = [jnp.sum(r[...], axis=0) for r in page_refs]
    for i in range(len(sums) // PAGES_PER_BLOCK):
        o_ref[i] = sums[2 * i] + sums[2 * i + 1]


def page_block_sums(cache_k, layer, page_table_flat, n_seq, n_pages):
    _, _, page, n_heads, hd = cache_k.shape
    pps = PAGES_PER_STEP
    assert n_pages % pps == 0
    n_blk = n_pages // PAGES_PER_BLOCK

    def page_spec(r):
        return pl.BlockSpec((None, None, page, n_heads, hd),
                            lambda b, g, pt: (layer, pt[b * n_pages + g * pps + r], 0, 0, 0))

    return pl.pallas_call(
        _page_sums_kernel,
        out_shape=jax.ShapeDtypeStruct((n_seq, n_blk, n_heads, hd), F32),
        grid_spec=pltpu.PrefetchScalarGridSpec(
            num_scalar_prefetch=1,
            grid=(n_seq, n_pages // pps),
            in_specs=[page_spec(r) for r in range(pps)],
            out_specs=pl.BlockSpec((None, pps // PAGES_PER_BLOCK, n_heads, hd),
                                   lambda b, g, pt: (b, g, 0, 0)),
        ),
        compiler_params=_params("parallel", "arbitrary"),
        name="page_block_sums",
    )(page_table_flat, *([cache_k] * pps))


def _block_choice_kernel(bs_ref, q_ref, o_ref):
    bs = bs_ref[...]
    gate = jnp.broadcast_to(jnp.sum(bs * q_ref[...], axis=2, keepdims=True), bs.shape)
    blk_id = lax.broadcasted_iota(jnp.int32, bs.shape, 0).astype(F32)
    o_ref[...] = jnp.zeros_like(o_ref)
    for s in range(MOBA_TOPK):
        mx = jnp.max(gate, axis=0, keepdims=True)
        idx = jnp.min(jnp.where(gate == mx, blk_id, 1e9), axis=0, keepdims=True)
        o_ref[s] = idx[0].astype(jnp.int32)
        gate = jnp.where(blk_id == idx, -jnp.inf, gate)


def block_choice(blk_sums, q_s):
    n_seq, n_blk, n_heads, hd = blk_sums.shape
    return pl.pallas_call(
        _block_choice_kernel,
        out_shape=jax.ShapeDtypeStruct((n_seq, 8, n_heads, hd), jnp.int32),
        grid=(n_seq,),
        in_specs=[
            pl.BlockSpec((None, n_blk, n_heads, hd), lambda b: (b, 0, 0, 0)),
            pl.BlockSpec((None, n_heads, hd), lambda b: (b, 0, 0)),
        ],
        out_specs=pl.BlockSpec((None, 8, n_heads, hd), lambda b: (b, 0, 0, 0)),
        compiler_params=_params("parallel"),
        name="block_choice",
    )(blk_sums, q_s.reshape(n_seq, n_heads, hd))


def _sample_attend_kernel(phys_ref, idx_ref, q_ref, kn_ref, vn_ref, ck_ref, cv_ref, o_ref,
                          kbuf, vbuf, sem, *, n_heads, past_len, layer):
    n_sel = MOBA_TOPK * PAGES_PER_BLOCK
    n_keys = n_sel * PAGE_SIZE
    b = pl.program_id(0)
    nb = pl.num_programs(0)
    hd = HEAD_DIM
    scale = hd ** -0.5
    slot = lax.rem(b, 2)

    def copies(seq, slot_):
        out = []
        for h in range(n_heads):
            for s in range(n_sel):
                page = phys_ref[(seq * n_heads + h) * n_sel + s]
                rows = pl.ds(s * PAGE_SIZE, PAGE_SIZE)
                out.append(pltpu.make_async_copy(ck_ref.at[layer, page, :, h, :],
                                                 kbuf.at[slot_, h, rows, :], sem.at[0, slot_]))
                out.append(pltpu.make_async_copy(cv_ref.at[layer, page, :, h, :],
                                                 vbuf.at[slot_, h, rows, :], sem.at[1, slot_]))
        return out

    @pl.when(b == 0)
    def _():
        for cp in copies(b, slot):
            cp.start()

    @pl.when(b + 1 < nb)
    def _():
        for cp in copies(b + 1, 1 - slot):
            cp.start()

    for cp in copies(b, slot):
        cp.wait()

    key_row = lax.broadcasted_iota(jnp.int32, (PAGE_SIZE, hd), 0)
    for h in range(n_heads):
        slope = 2.0 ** (-8.0 * (h + 1) / n_heads)
        q = q_ref[h:h + 1, :]
        qk = lax.dot_general(kbuf[slot, h], jnp.broadcast_to(q, (hd, hd)), (((1,), (1,)), ((), ())),
                             preferred_element_type=F32, precision=lax.Precision.HIGHEST)
        dists = []
        for s in range(n_sel):
            blk = idx_ref[(b * n_heads + h) * MOBA_TOPK + s // PAGES_PER_BLOCK]
            pos0 = blk * MOBA_BLOCK + (s % PAGES_PER_BLOCK) * PAGE_SIZE
            dists.append((past_len - pos0 - key_row).astype(F32))
        sc = qk * scale - slope * jnp.concatenate(dists, axis=0)
        s_self = jnp.sum(q * kn_ref[h:h + 1, :], axis=1, keepdims=True) * scale
        m = jnp.maximum(jnp.max(sc, axis=0, keepdims=True), s_self)
        p = jnp.exp(sc - m)
        p_self = jnp.exp(s_self - m)
        l = jnp.sum(p, axis=0, keepdims=True) + p_self
        acc = jnp.sum(p * vbuf[slot, h], axis=0, keepdims=True) + p_self * vn_ref[h:h + 1, :]
        o_ref[h:h + 1, :] = acc / l


def sample_attend(q_s, k_s, v_s, cache_k, cache_v, layer, phys, idx_flat, past_len):
    n_seq, width = q_s.shape
    hd = HEAD_DIM
    n_heads = width // hd
    n_keys = MOBA_TOPK * MOBA_BLOCK
    row_spec = pl.BlockSpec((None, n_heads, hd), lambda b, ph, ix: (b, 0, 0))
    to3 = lambda a: a.reshape(n_seq, n_heads, hd)
    out = pl.pallas_call(
        functools.partial(_sample_attend_kernel, n_heads=n_heads, past_len=past_len, layer=layer),
        out_shape=jax.ShapeDtypeStruct((n_seq, n_heads, hd), F32),
        grid_spec=pltpu.PrefetchScalarGridSpec(
            num_scalar_prefetch=2,
            grid=(n_seq,),
            in_specs=[row_spec, row_spec, row_spec,
                      pl.BlockSpec(memory_space=pl.ANY), pl.BlockSpec(memory_space=pl.ANY)],
            out_specs=row_spec,
            scratch_shapes=[
                pltpu.VMEM((2, n_heads, n_keys, hd), F32),
                pltpu.VMEM((2, n_heads, n_keys, hd), F32),
                pltpu.SemaphoreType.DMA((2, 2)),
            ],
        ),
        compiler_params=_params("arbitrary"),
        name="sample_attend",
    )(phys, idx_flat, to3(q_s), to3(k_s), to3(v_s), cache_k, cache_v)
    return out.reshape(n_seq, width)


def _sgu_merge_kernel(x_ref, att_ref, u_ref, vg_ref, ga_ref, gb_ref, gs_ref, ws_ref, bs_ref,
                      pa_ref, pb_ref, wo_ref, *refs, chunked):
    if chunked:
        o_ref, m_ref, acc_ref = refs
    else:
        o_ref, vn_out_ref, m_ref, acc_ref = refs
    j = pl.program_id(1)
    tm = x_ref.shape[0]

    @pl.when(j == 0)
    def _():
        vn = _rms_rows(vg_ref[...].astype(F32), gs_ref[...])
        if chunked:
            ch = SGU_CHUNK
            gd = SGU_GROUP_DIM
            r = lax.broadcasted_iota(jnp.int32, (ch, ch), 0)
            c = lax.broadcasted_iota(jnp.int32, (ch, ch), 1)
            vnb = vn.astype(BF16)
            for g in range(ws_ref.shape[0]):
                w = jnp.where(c <= r, ws_ref[g], 0.0).astype(BF16)
                bias = bs_ref[g]
                for cc in range(tm // ch):
                    rows = slice(cc * ch, (cc + 1) * ch)
                    cols = slice(g * gd, (g + 1) * gd)
                    z = jnp.dot(w, vnb[rows, cols], preferred_element_type=F32) + bias
                    m_ref[rows, cols] = (u_ref[rows, cols].astype(F32) * z).astype(m_ref.dtype)
        else:
            vn_out_ref[...] = vn
            z = vn * ws_ref[...] + bs_ref[...]
            m_ref[...] = (u_ref[...].astype(F32) * z).astype(m_ref.dtype)

    pa = jnp.dot(att_ref[...].astype(BF16), pa_ref[...], preferred_element_type=F32)
    pb = jnp.dot(m_ref[...], pb_ref[...], preferred_element_type=F32)
    mixed = _sigmoid(ga_ref[...].astype(F32)) * pa + _sigmoid(gb_ref[...].astype(F32)) * pb
    part = jnp.dot(mixed.astype(BF16), wo_ref[...], preferred_element_type=F32)

    @pl.when(j == 0)
    def _():
        acc_ref[...] = part

    @pl.when(j != 0)
    def _():
        acc_ref[...] += part

    @pl.when(j == pl.num_programs(1) - 1)
    def _():
        o_ref[...] = x_ref[...] + acc_ref[...]


def sgu_merge(x, att, rest, g_sgu, w_sgu, b_sgu, w_pa, w_pb, w_o, chunked, tm, tn):
    t, d = x.shape
    a_w = att.shape[1]
    b_w = g_sgu.shape[0]
    tm = min(tm, t)
    assert d % tn == 0 and b_w % tn == 0 and a_w == b_w and tn == b_w
    nj = d // tn
    n_groups = w_sgu.shape[0]
    if chunked:
        assert tm % SGU_CHUNK == 0
        ws = w_sgu
        bs = jnp.broadcast_to(b_sgu[:, :, None], (n_groups, SGU_CHUNK, SGU_GROUP_DIM))
        ws_spec = pl.BlockSpec((n_groups, SGU_CHUNK, SGU_CHUNK), lambda i, j: (0, 0, 0))
        bs_spec = pl.BlockSpec((n_groups, SGU_CHUNK, SGU_GROUP_DIM), lambda i, j: (0, 0, 0))
    else:
        ws = jnp.repeat(w_sgu[:, 0, 0], SGU_GROUP_DIM).reshape(1, b_w)
        bs = jnp.repeat(b_sgu[:, 0], SGU_GROUP_DIM).reshape(1, b_w)
        ws_spec = bs_spec = pl.BlockSpec((1, b_w), lambda i, j: (0, 0))
    u_blk = b_w // tn
    in_specs = [
        pl.BlockSpec((tm, d), lambda i, j: (i, 0)),
        pl.BlockSpec((tm, a_w), lambda i, j: (i, 0)),
        pl.BlockSpec((tm, b_w), lambda i, j: (i, 0)),
        pl.BlockSpec((tm, b_w), lambda i, j: (i, 1)),
        pl.BlockSpec((tm, tn), lambda i, j: (i, 2 * u_blk + j)),
        pl.BlockSpec((tm, tn), lambda i, j: (i, 2 * u_blk + nj + j)),
        pl.BlockSpec((1, b_w), lambda i, j: (0, 0)),
        ws_spec, bs_spec,
        pl.BlockSpec((a_w, tn), lambda i, j: (0, j)),
        pl.BlockSpec((b_w, tn), lambda i, j: (0, j)),
        pl.BlockSpec((tn, d), lambda i, j: (j, 0)),
    ]
    x_spec = pl.BlockSpec((tm, d), lambda i, j: (i, 0))
    if chunked:
        out_shape = jax.ShapeDtypeStruct((t, d), F32)
        out_specs = x_spec
    else:
        out_shape = (jax.ShapeDtypeStruct((t, d), F32), jax.ShapeDtypeStruct((t, b_w), F32))
        out_specs = (x_spec, pl.BlockSpec((tm, b_w), lambda i, j: (i, 0)))
    return pl.pallas_call(
        functools.partial(_sgu_merge_kernel, chunked=chunked),
        out_shape=out_shape,
        grid=(t // tm, nj),
        in_specs=in_specs,
        out_specs=out_specs,
        scratch_shapes=[pltpu.VMEM((tm, b_w), BF16), pltpu.VMEM((tm, d), F32)],
        compiler_params=_params("parallel", "arbitrary"),
        name="sgu_merge",
    )(x, att, rest, rest, rest, rest, g_sgu.reshape(1, b_w), ws, bs, w_pa, w_pb, w_o)


def _moe_route_kernel(x_ref, g_ref, wr_ref, br_ref, xn_ref, comb_ref, *, as_list):
    xn = _rms_rows(x_ref[...], g_ref[...])
    xn_ref[...] = xn.astype(xn_ref.dtype)
    logits = jnp.dot(xn, wr_ref[...], preferred_element_type=F32,
                     precision=lax.Precision.HIGHEST) + br_ref[...]
    lane = lax.broadcasted_iota(jnp.int32, logits.shape, 1)
    lanef = lane.astype(F32)
    e0 = ROUTER_EXPERT_LANE0
    is_grp = lane < e0
    gl = jnp.where(is_grp, logits, -jnp.inf)
    gmax = jnp.max(gl, axis=1, keepdims=True)
    grp = jnp.min(jnp.where(gl == gmax, lanef, 1e9), axis=1, keepdims=True)
    p_grp = 1.0 / jnp.sum(jnp.where(is_grp, jnp.exp(logits - gmax), 0.0), axis=1, keepdims=True)
    lane_grp = jnp.right_shift(lane - e0, 2).astype(F32)
    in_grp = (lane >= e0) & (lane < e0 + N_EXPERTS) & (lane_grp == grp)
    el = jnp.where(in_grp, logits, -jnp.inf)
    t1 = jnp.max(el, axis=1, keepdims=True)
    i1 = jnp.min(jnp.where(el == t1, lanef, 1e9), axis=1, keepdims=True)
    el2 = jnp.where(lanef == i1, -jnp.inf, el)
    t2 = jnp.max(el2, axis=1, keepdims=True)
    i2 = jnp.min(jnp.where(el2 == t2, lanef, 1e9), axis=1, keepdims=True)
    e2 = jnp.exp(t2 - t1)
    w1 = (1.0 / (1.0 + e2)) * p_grp
    w2 = (e2 / (1.0 + e2)) * p_grp
    if as_list:
        comb_ref[...] = jnp.where(lane == 0, i1 - e0, jnp.where(lane == 1, i2 - e0,
                                  jnp.where(lane == 2, w1, jnp.where(lane == 3, w2, 0.0))))
    else:
        comb_ref[...] = jnp.where(lanef == i1, w1, jnp.where(lanef == i2, w2, 0.0))


def moe_route(x, g_ffn, w_rg, b_rg, w_re, b_re, tm, as_list):
    t, d = x.shape
    tm = min(tm, t)
    assert EXP_PER_GROUP == 4 and w_rg.shape[1] == N_EXP_GROUPS and w_re.shape[1] == N_EXPERTS
    pad = LANES - N_EXP_GROUPS - N_EXPERTS
    wr = jnp.concatenate([w_rg, w_re, jnp.zeros((d, pad), F32)], axis=1)
    br = jnp.concatenate([b_rg, b_re, jnp.zeros((pad,), F32)]).reshape(1, LANES)
    return pl.pallas_call(
        functools.partial(_moe_route_kernel, as_list=as_list),
        out_shape=(jax.ShapeDtypeStruct((t, d), F32 if as_list else BF16),
                   jax.ShapeDtypeStruct((t, LANES), F32)),
        grid=(t // tm,),
        in_specs=[
            pl.BlockSpec((tm, d), lambda i: (i, 0)),
            pl.BlockSpec((1, d), lambda i: (0, 0)),
            pl.BlockSpec((d, LANES), lambda i: (0, 0)),
            pl.BlockSpec((1, LANES), lambda i: (0, 0)),
        ],
        out_specs=(pl.BlockSpec((tm, d), lambda i: (i, 0)), pl.BlockSpec((tm, LANES), lambda i: (i, 0))),
        compiler_params=_params("parallel"),
        name="moe_route",
    )(x, g_ffn.reshape(1, d), wr, br)


def _moe_experts_kernel(xn_ref, comb_ref, x_ref, wg_ref, wu_ref, wd_ref, gf_ref, o_ref, acc_ref):
    e = pl.program_id(1)
    lane = lax.broadcasted_iota(jnp.int32, comb_ref.shape, 1)
    c = jnp.sum(jnp.where(lane == e + ROUTER_EXPERT_LANE0, comb_ref[...], 0.0), axis=1, keepdims=True)
    xn = xn_ref[...]
    g = jnp.dot(xn, wg_ref[...], preferred_element_type=F32)
    u = jnp.dot(xn, wu_ref[...], preferred_element_type=F32)
    h = (g * _sigmoid(g)) * u
    part = jnp.dot((h * c).astype(BF16), wd_ref[...], preferred_element_type=F32)

    @pl.when(e == 0)
    def _():
        acc_ref[...] = part

    @pl.when(e != 0)
    def _():
        acc_ref[...] += part

    @pl.when(e == pl.num_programs(1) - 1)
    def _():
        o_ref[...] = _rms_rows(x_ref[...] + acc_ref[...], gf_ref[...])


def moe_experts(xn, comb, x, w_gate, w_up, w_down, g_final, tm):
    t, d = x.shape
    n_exp, _, f = w_gate.shape
    tm = min(tm, t)
    return pl.pallas_call(
        _moe_experts_kernel,
        out_shape=jax.ShapeDtypeStruct((t, d), F32),
        grid=(t // tm, n_exp),
        in_specs=[
            pl.BlockSpec((tm, d), lambda i, e: (i, 0)),
            pl.BlockSpec((tm, LANES), lambda i, e: (i, 0)),
            pl.BlockSpec((tm, d), lambda i, e: (i, 0)),
            pl.BlockSpec((None, d, f), lambda i, e: (e, 0, 0)),
            pl.BlockSpec((None, d, f), lambda i, e: (e, 0, 0)),
            pl.BlockSpec((None, f, d), lambda i, e: (e, 0, 0)),
            pl.BlockSpec((1, d), lambda i, e: (0, 0)),
        ],
        out_specs=pl.BlockSpec((tm, d), lambda i, e: (i, 0)),
        scratch_shapes=[pltpu.VMEM((tm, d), F32)],
        compiler_params=_params("parallel", "arbitrary"),
        name="moe_experts",
    )(xn, comb, x, w_gate, w_up, w_down, g_final.reshape(1, d))


def _dispatch_plan(e_tok, w_tok, tm):
    n_pairs = e_tok.size
    n_exp = N_EXPERTS
    ef = e_tok.reshape(-1)
    onehot = (ef[:, None] == jnp.arange(n_exp, dtype=jnp.int32)[None, :]).astype(jnp.int32)
    counts = jnp.sum(onehot, axis=0)
    rank = jnp.sum((jnp.cumsum(onehot, axis=0) - onehot) * onehot, axis=1)
    padded = ((counts + tm - 1) // tm) * tm
    g_end = jnp.cumsum(padded)
    g_start = g_end - padded
    c_start = jnp.cumsum(counts) - counts
    pair_row = g_start[ef] + rank
    n_rows = n_pairs + n_exp * tm
    n_tiles = n_rows // tm
    order = jnp.argsort(ef, stable=True).astype(jnp.int32)
    tile_expert = jnp.minimum(jnp.searchsorted(g_end, jnp.arange(n_tiles, dtype=jnp.int32) * tm,
                                               side="right"), n_exp - 1).astype(jnp.int32)
    row = jnp.arange(n_rows, dtype=jnp.int32)
    row_e = jnp.repeat(tile_expert, tm)
    r_in = row - g_start[row_e]
    valid = (r_in < counts[row_e]) & (row < g_end[n_exp - 1])
    src_pair = order[jnp.clip(c_start[row_e] + r_in, 0, n_pairs - 1)]
    row_token = jnp.where(valid, src_pair // 2, 0).astype(jnp.int32)
    row_w = jnp.where(valid, w_tok.reshape(-1)[src_pair], 0.0).astype(F32)
    tiles_used = (g_end[n_exp - 1] // tm).astype(jnp.int32).reshape(1)
    return row_token, row_w.reshape(n_rows, 1), tile_expert, tiles_used, pair_row.astype(jnp.int32)


def _row_gather(src_hbm, dst, sem, row_ids, base, n_rows, slot):
    def body(r, carry):
        tok = row_ids[base + r]
        pltpu.make_async_copy(src_hbm.at[pl.ds(tok, 1), :], dst.at[slot, pl.ds(r, 1), :], sem.at[slot]).start()
        return carry
    lax.fori_loop(0, n_rows, body, 0, unroll=8)


def _row_gather_wait(src_hbm, dst, sem, n_rows, slot):
    def body(r, carry):
        pltpu.make_async_copy(src_hbm.at[pl.ds(0, 1), :], dst.at[slot, pl.ds(r, 1), :], sem.at[slot]).wait()
        return carry
    lax.fori_loop(0, n_rows, body, 0, unroll=8)


def _moe_sparse_kernel(tok_ref, texp_ref, used_ref, xn_hbm, w_ref, wg_ref, wu_ref, wd_ref, o_ref,
                       xbuf, sem):
    del texp_ref
    i = pl.program_id(0)
    tm = o_ref.shape[0]
    used = used_ref[0]
    slot = lax.rem(i, 2)

    @pl.when((i == 0) & (used > 0))
    def _():
        _row_gather(xn_hbm, xbuf, sem, tok_ref, 0, tm, 0)

    @pl.when(i + 1 < used)
    def _():
        _row_gather(xn_hbm, xbuf, sem, tok_ref, (i + 1) * tm, tm, 1 - slot)

    @pl.when(i < used)
    def _():
        _row_gather_wait(xn_hbm, xbuf, sem, tm, slot)
        xs = xbuf[slot].astype(BF16)
        g = jnp.dot(xs, wg_ref[...], preferred_element_type=F32)
        u = jnp.dot(xs, wu_ref[...], preferred_element_type=F32)
        h = (g * _sigmoid(g)) * u
        o_ref[...] = jnp.dot((h * w_ref[...]).astype(BF16), wd_ref[...], preferred_element_type=F32)

    @pl.when(i >= used)
    def _():
        o_ref[...] = jnp.zeros_like(o_ref)


def moe_sparse_experts(xn, row_token, row_w, tile_expert, tiles_used, w_gate, w_up, w_down, tm):
    t, d = xn.shape
    n_exp, _, f = w_gate.shape
    n_rows = row_token.shape[0]
    return pl.pallas_call(
        _moe_sparse_kernel,
        out_shape=jax.ShapeDtypeStruct((n_rows, d), F32),
        grid_spec=pltpu.PrefetchScalarGridSpec(
            num_scalar_prefetch=3,
            grid=(n_rows // tm,),
            in_specs=[
                pl.BlockSpec(memory_space=pl.ANY),
                pl.BlockSpec((tm, 1), lambda i, tok, te, nu: (i, 0)),
                pl.BlockSpec((None, d, f), lambda i, tok, te, nu: (te[i], 0, 0)),
                pl.BlockSpec((None, d, f), lambda i, tok, te, nu: (te[i], 0, 0)),
                pl.BlockSpec((None, f, d), lambda i, tok, te, nu: (te[i], 0, 0)),
            ],
            out_specs=pl.BlockSpec((tm, d), lambda i, tok, te, nu: (i, 0)),
            scratch_shapes=[pltpu.VMEM((2, tm, d), F32), pltpu.SemaphoreType.DMA((2,))],
        ),
        compiler_params=_params("arbitrary"),
        name="moe_sparse_experts",
    )(row_token, tile_expert, tiles_used, xn, row_w, w_gate, w_up, w_down)


def _moe_combine_kernel(row_ref, ys_hbm, x_ref, gf_ref, o_ref, ybuf, sem):
    i = pl.program_id(0)
    n = pl.num_programs(0)
    tm = o_ref.shape[0]
    slot = lax.rem(i, 2)

    @pl.when(i == 0)
    def _():
        _row_gather(ys_hbm, ybuf, sem, row_ref, 0, 2 * tm, 0)

    @pl.when(i + 1 < n)
    def _():
        _row_gather(ys_hbm, ybuf, sem, row_ref, (i + 1) * 2 * tm, 2 * tm, 1 - slot)

    _row_gather_wait(ys_hbm, ybuf, sem, 2 * tm, slot)
    o_ref[...] = _rms_rows(x_ref[...] + (ybuf[slot, 0:tm, :] + ybuf[slot, tm:2 * tm, :]), gf_ref[...])


def moe_combine(ys, pair_row, x, g_final, tm):
    t, d = x.shape
    rows = jnp.transpose(pair_row.reshape(t // tm, tm, 2), (0, 2, 1)).reshape(-1)
    return pl.pallas_call(
        _moe_combine_kernel,
        out_shape=jax.ShapeDtypeStruct((t, d), F32),
        grid_spec=pltpu.PrefetchScalarGridSpec(
            num_scalar_prefetch=1,
            grid=(t // tm,),
            in_specs=[
                pl.BlockSpec(memory_space=pl.ANY),
                pl.BlockSpec((tm, d), lambda i, rw: (i, 0)),
                pl.BlockSpec((1, d), lambda i, rw: (0, 0)),
            ],
            out_specs=pl.BlockSpec((tm, d), lambda i, rw: (i, 0)),
            scratch_shapes=[pltpu.VMEM((2, 2 * tm, d), F32), pltpu.SemaphoreType.DMA((2,))],
        ),
        compiler_params=_params("arbitrary"),
        name="moe_combine",
    )(rows, ys, x, g_final.reshape(1, d))


def kernel(x_prompt, x_sample, cache_k, cache_v, page_table, g_mix, w_in, g_sgu, w_sgu, b_sgu, w_proj_a, w_proj_b, w_out, g_ffn, w_router_group, b_router_group, w_router_expert, b_router_expert, w_exp_gate, w_exp_up, w_exp_down, g_final):
    depth = g_mix.shape[0]
    assert depth == 1
    layer = 0
    b_sz, s_len, d = x_prompt.shape
    n_seq, n_new, _ = x_sample.shape
    assert n_new == 1
    _, n_pool, page, n_heads, hd = cache_k.shape
    assert page == PAGE_SIZE and hd == HEAD_DIM
    a_w = n_heads * hd
    b_w = g_sgu.shape[1]
    n_pages = page_table.shape[1]
    past_len = n_pages * PAGE_SIZE
    assert past_len % MOBA_BLOCK == 0 and n_pages // PAGES_PER_BLOCK >= MOBA_TOPK

    w_in_l = w_in[layer]
    w_q = w_in_l[:, 0:a_w].astype(BF16)
    w_k = w_in_l[:, a_w:2 * a_w].astype(BF16)
    w_v = w_in_l[:, 2 * a_w:3 * a_w].astype(BF16)
    w_rest = w_in_l[:, 3 * a_w:].astype(BF16)
    w_pa = w_proj_a[layer].astype(BF16)
    w_pb = w_proj_b[layer].astype(BF16)
    w_o = w_out[layer].astype(BF16)
    w_eg = w_exp_gate[layer].astype(BF16)
    w_eu = w_exp_up[layer].astype(BF16)
    w_ed = w_exp_down[layer].astype(BF16)

    def dense_tail(x2, att, rest, chunked):
        merged = sgu_merge(x2, att, rest, g_sgu[layer], w_sgu[layer], b_sgu[layer], w_pa, w_pb, w_o,
                           chunked=chunked, tm=512, tn=1024)
        x1, vn = (merged, None) if chunked else merged
        sparse = x1.shape[0] >= MOE_SPARSE_MIN_TOKENS
        xn, route = moe_route(x1, g_ffn[layer], w_router_group[layer], b_router_group[layer],
                              w_router_expert[layer], b_router_expert[layer], tm=256, as_list=sparse)
        if sparse:
            e_tok = route[:, 0:TOP_K_IN_GROUP].astype(jnp.int32)
            w_tok = route[:, TOP_K_IN_GROUP:2 * TOP_K_IN_GROUP]
            row_token, row_w, tile_expert, tiles_used, pair_row = _dispatch_plan(e_tok, w_tok, MOE_ROW_TILE)
            ys = moe_sparse_experts(xn, row_token, row_w, tile_expert, tiles_used, w_eg, w_eu, w_ed,
                                    tm=MOE_ROW_TILE)
            y = moe_combine(ys, pair_row.reshape(-1, TOP_K_IN_GROUP), x1, g_final, tm=MOE_ROW_TILE)
        else:
            y = moe_experts(xn, route, x1, w_eg, w_eu, w_ed, g_final, tm=512)
        return y, vn

    xp = x_prompt.reshape(b_sz * s_len, d)
    g_in = g_mix[layer]
    (q_p,) = norm_matmul(xp, g_in, w_q, (F32,), tm=1024, tn=1024)
    k_p, kb_p = norm_matmul(xp, g_in, w_k, (F32, BF16), tm=1024, tn=1024)
    v_p, vb_p = norm_matmul(xp, g_in, w_v, (F32, BF16), tm=1024, tn=1024)
    (rest_p,) = norm_matmul(xp, g_in, w_rest, (BF16,), tm=1024, tn=1024)
    att_p = moba_prompt(q_p, kb_p, vb_p, block_means(k_p), b_sz, s_len, BF16)
    y_p, _ = dense_tail(xp, att_p, rest_p, chunked=True)

    xs = x_sample.reshape(n_seq, d)
    (q_s,) = norm_matmul(xs, g_in, w_q, (F32,), tm=n_seq, tn=1024)
    (k_s,) = norm_matmul(xs, g_in, w_k, (F32,), tm=n_seq, tn=1024)
    (v_s,) = norm_matmul(xs, g_in, w_v, (F32,), tm=n_seq, tn=1024)
    (rest_s,) = norm_matmul(xs, g_in, w_rest, (F32,), tm=n_seq, tn=1024)
    pt_flat = page_table.reshape(-1)
    blk_sums = page_block_sums(cache_k, layer, pt_flat, n_seq, n_pages)
    picks = block_choice(blk_sums, q_s)
    idx = jnp.transpose(picks[:, :MOBA_TOPK, :, 0], (0, 2, 1))
    pages = idx[..., None] * PAGES_PER_BLOCK + jnp.arange(PAGES_PER_BLOCK, dtype=jnp.int32)
    phys = jnp.take_along_axis(page_table[:, None, :], pages.reshape(n_seq, n_heads, -1), axis=2)
    att_s = sample_attend(q_s, k_s, v_s, cache_k, cache_v, layer, phys.reshape(-1), idx.reshape(-1),
                          past_len)
    y_s, vn_s = dense_tail(xs, att_s, rest_s, chunked=False)

    return (
        y_p.reshape(b_sz, s_len, d),
        y_s.reshape(n_seq, n_new, d),
        k_p.reshape(depth, b_sz, s_len, n_heads, hd),
        v_p.reshape(depth, b_sz, s_len, n_heads, hd),
        k_s.reshape(depth, n_seq, n_new, n_heads, hd),
        v_s.reshape(depth, n_seq, n_new, n_heads, hd),
        vn_s.reshape(depth, n_seq, n_new, b_w),
    )
```

```python
import functools

import jax
import jax.numpy as jnp
from jax import lax
from jax.experimental import pallas as pl
from jax.experimental.pallas import tpu as pltpu

NORM_EPS = 1e-6
NEG_INF = -1e30
HEAD_DIM = 128
MOBA_BLOCK = 256
MOBA_TOPK = 3
PAGE_SIZE = 128
PAGES_PER_BLOCK = MOBA_BLOCK // PAGE_SIZE
SGU_CHUNK = 128
SGU_GROUP_DIM = 128
N_EXP_GROUPS = 4
EXP_PER_GROUP = 4
N_EXPERTS = N_EXP_GROUPS * EXP_PER_GROUP
TOP_K_IN_GROUP = 2
MOE_ROW_TILE = 256
MOE_SPARSE_MIN_TOKENS = 2048
MOE_MATMUL_PIECE = 256
LANES = 128
ROUTER_EXPERT_LANE0 = N_EXP_GROUPS
VMEM_LIMIT_BYTES = 56 * 1024 * 1024
PAGES_PER_STEP = 16
MOBA_HEADS_PER_STEP = 8
LOG2E = 1.4426950408889634

F32 = jnp.float32
BF16 = jnp.bfloat16


def _params(*semantics):
    return pltpu.CompilerParams(dimension_semantics=semantics, vmem_limit_bytes=VMEM_LIMIT_BYTES)


def _rms_rows(x, g):
    inv = lax.rsqrt(jnp.mean(x * x, axis=-1, keepdims=True) + NORM_EPS)
    return (x * inv) * g


def _sigmoid(x):
    return 1.0 / (1.0 + jnp.exp(-x))


def _norm_matmul_kernel(x_ref, g_ref, w_ref, *refs, row_chunk):
    out_refs, xn_ref = refs[:-1], refs[-1]

    @pl.when(pl.program_id(1) == 0)
    def _():
        def body(r, carry):
            rows = pl.ds(pl.multiple_of(r * row_chunk, row_chunk), row_chunk)
            xn_ref[rows, :] = _rms_rows(x_ref[rows, :], g_ref[...]).astype(xn_ref.dtype)
            return carry
        lax.fori_loop(0, x_ref.shape[0] // row_chunk, body, 0)

    res = jnp.dot(xn_ref[...], w_ref[...], preferred_element_type=F32)
    for o_ref in out_refs:
        o_ref[...] = res.astype(o_ref.dtype)


def _norm_qkv_kernel(x_ref, g_ref, w_ref, q_ref, k_ref, v_ref, kb_ref, vb_ref, xn_ref, *, row_chunk):
    j = pl.program_id(1)

    @pl.when(j == 0)
    def _():
        def body(r, carry):
            rows = pl.ds(pl.multiple_of(r * row_chunk, row_chunk), row_chunk)
            xn_ref[rows, :] = _rms_rows(x_ref[rows, :], g_ref[...]).astype(xn_ref.dtype)
            return carry
        lax.fori_loop(0, x_ref.shape[0] // row_chunk, body, 0)

    res = jnp.dot(xn_ref[...], w_ref[...], preferred_element_type=F32)

    @pl.when(j == 0)
    def _():
        q_ref[...] = res

    @pl.when(j == 1)
    def _():
        k_ref[...] = res
        kb_ref[...] = res.astype(kb_ref.dtype)

    @pl.when(j == 2)
    def _():
        v_ref[...] = res
        vb_ref[...] = res.astype(vb_ref.dtype)


def norm_qkv(x, g, w_qkv, tm):
    t, d = x.shape
    a_w = w_qkv.shape[1] // 3
    tm = min(tm, t)
    row_chunk = min(tm, 128)
    out_spec = pl.BlockSpec((tm, a_w), lambda i, j: (i, 0))
    return pl.pallas_call(
        functools.partial(_norm_qkv_kernel, row_chunk=row_chunk),
        out_shape=tuple(jax.ShapeDtypeStruct((t, a_w), dt) for dt in (F32, F32, F32, BF16, BF16)),
        grid=(t // tm, 3),
        in_specs=[
            pl.BlockSpec((tm, d), lambda i, j: (i, 0)),
            pl.BlockSpec((1, d), lambda i, j: (0, 0)),
            pl.BlockSpec((d, a_w), lambda i, j: (0, j)),
        ],
        out_specs=tuple(out_spec for _ in range(5)),
        scratch_shapes=[pltpu.VMEM((tm, d), BF16)],
        compiler_params=_params("parallel", "arbitrary"),
        name="norm_qkv",
    )(x, g.reshape(1, d), w_qkv)


def norm_matmul(x, g, w, out_dtypes, tm, tn):
    t, d = x.shape
    n = w.shape[1]
    tm = min(tm, t)
    tn = min(tn, n)
    row_chunk = min(tm, 128)
    out_spec = pl.BlockSpec((tm, tn), lambda i, j: (i, j))
    return pl.pallas_call(
        functools.partial(_norm_matmul_kernel, row_chunk=row_chunk),
        out_shape=tuple(jax.ShapeDtypeStruct((t, n), dt) for dt in out_dtypes),
        grid=(t // tm, n // tn),
        in_specs=[
            pl.BlockSpec((tm, d), lambda i, j: (i, 0)),
            pl.BlockSpec((1, d), lambda i, j: (0, 0)),
            pl.BlockSpec((d, tn), lambda i, j: (0, j)),
        ],
        out_specs=tuple(out_spec for _ in out_dtypes),
        scratch_shapes=[pltpu.VMEM((tm, d), BF16)],
        compiler_params=_params("parallel", "arbitrary"),
        name="norm_matmul",
    )(x, g.reshape(1, d), w)


def _block_means_kernel(k_ref, o_ref):
    blk = MOBA_BLOCK
    for n in range(o_ref.shape[0]):
        o_ref[n:n + 1, :] = jnp.sum(k_ref[n * blk:(n + 1) * blk, :], axis=0, keepdims=True) * (1.0 / blk)


def block_means(k):
    t, width = k.shape
    per_step = 8
    rows = per_step * MOBA_BLOCK
    assert t % rows == 0
    return pl.pallas_call(
        _block_means_kernel,
        out_shape=jax.ShapeDtypeStruct((t // MOBA_BLOCK, width), F32),
        grid=(t // rows,),
        in_specs=[pl.BlockSpec((rows, width), lambda i: (i, 0))],
        out_specs=pl.BlockSpec((per_step, width), lambda i: (i, 0)),
        compiler_params=_params("parallel"),
        name="block_means",
    )(k)


def _moba_prompt_kernel(q_ref, k_ref, v_ref, kmean_ref, o_ref, e_ref, bias_ref, biasd_ref, gate_ref,
                        *, n_blk, n_heads, hp):
    blk = MOBA_BLOCK
    hd = HEAD_DIM
    grp = pl.program_id(1)
    qi = pl.program_id(2)
    c2 = hd ** -0.5 * LOG2E
    nt = (((1,), (1,)), ((), ()))
    slopes = [jnp.exp2(jnp.full((1, 1), -8.0 / n_heads, F32) * (grp * hp + t + 1).astype(F32)) * LOG2E
              for t in range(hp)]

    @pl.when(qi == 0)
    def _():
        lane = lax.broadcasted_iota(jnp.int32, (blk, hd), 1)
        for n in range(n_blk):
            e_ref[n * blk:(n + 1) * blk, :] = jnp.where(lane == n, 1.0, 0.0).astype(BF16)
        r = lax.broadcasted_iota(jnp.int32, (blk, blk), 0)
        c = lax.broadcasted_iota(jnp.int32, (blk, blk), 1)
        for t in range(hp):
            alibi = -slopes[t] * (r - c).astype(F32)
            bias_ref[t] = alibi
            biasd_ref[t] = jnp.where(c <= r, alibi, NEG_INF)

    own = pl.ds(pl.multiple_of(qi * blk, blk), blk)
    q_augs, carry0 = [], []
    for t in range(hp):
        cols = slice(t * hd, (t + 1) * hd)
        q = q_ref[:, cols]
        qb = q.astype(BF16)
        gate_ref[t, 0:n_blk, :] = lax.dot_general(kmean_ref[:, cols], q, nt, preferred_element_type=F32,
                                                  precision=lax.Precision.HIGHEST)
        g_all = gate_ref[t, 0:n_blk, :]
        row = lax.broadcasted_iota(jnp.int32, g_all.shape, 0)
        rank = jnp.zeros(g_all.shape, jnp.int32)
        for j in range(n_blk):
            gj = gate_ref[t, j:j + 1, :]
            ahead = (gj > g_all) | ((gj == g_all) & (j < row))
            rank = rank + jnp.where(ahead & (j < qi), 1, 0)
        chosen = (row < qi) & (rank < MOBA_TOPK)
        gate_ref[t] = jnp.zeros(gate_ref.shape[1:], F32)
        gate_ref[t, 0:n_blk, :] = jnp.where(chosen, 0.0, NEG_INF)
        q_augs.append(jnp.concatenate([qb, gate_ref[t].T.astype(BF16)], axis=1))

        s = lax.dot_general(qb, k_ref[own, cols], nt, preferred_element_type=F32) * c2 + biasd_ref[t]
        m0 = jnp.max(s, axis=1, keepdims=True)
        p = jnp.exp2(s - m0)
        l0 = jnp.sum(p, axis=1, keepdims=True)
        acc0 = jnp.dot(p.astype(BF16), v_ref[own, cols], preferred_element_type=F32)
        carry0 += [m0, l0, acc0]

    def body(j, carry):
        rows = pl.ds(pl.multiple_of(j * blk, blk), blk)
        dist = (blk * (qi - j)).astype(F32)
        e_j = e_ref[rows, :]
        out = []
        for t in range(hp):
            m, l, acc = carry[3 * t:3 * t + 3]
            cols = slice(t * hd, (t + 1) * hd)
            k_aug = jnp.concatenate([k_ref[rows, cols], e_j], axis=1)
            s = lax.dot_general(q_augs[t], k_aug, nt, preferred_element_type=F32) * c2 + bias_ref[t]
            off = -slopes[t] * dist
            m_new = jnp.maximum(m, jnp.max(s, axis=1, keepdims=True) + off)
            alpha = jnp.exp2(m - m_new)
            p = jnp.exp2(s - (m_new - off))
            l = alpha * l + jnp.sum(p, axis=1, keepdims=True)
            acc = alpha * acc + jnp.dot(p.astype(BF16), v_ref[rows, cols], preferred_element_type=F32)
            out += [m_new, l, acc]
        return tuple(out)

    final = lax.fori_loop(0, qi, body, tuple(carry0))
    for t in range(hp):
        _, l, acc = final[3 * t:3 * t + 3]
        o_ref[:, t * hd:(t + 1) * hd] = (acc / l).astype(o_ref.dtype)


def moba_prompt(q, k_bf, v_bf, kmeans, b_sz, s_len, out_dtype):
    t, width = q.shape
    hd = HEAD_DIM
    n_heads = width // hd
    hp = MOBA_HEADS_PER_STEP
    blk = MOBA_BLOCK
    n_blk = s_len // blk
    assert s_len % blk == 0 and n_blk % 8 == 0 and n_blk <= LANES and n_heads % hp == 0
    return pl.pallas_call(
        functools.partial(_moba_prompt_kernel, n_blk=n_blk, n_heads=n_heads, hp=hp),
        out_shape=jax.ShapeDtypeStruct((t, width), out_dtype),
        grid=(b_sz, n_heads // hp, n_blk),
        in_specs=[
            pl.BlockSpec((blk, hp * hd), lambda b, g, i: (b * n_blk + i, g)),
            pl.BlockSpec((s_len, hp * hd), lambda b, g, i: (b, g)),
            pl.BlockSpec((s_len, hp * hd), lambda b, g, i: (b, g)),
            pl.BlockSpec((n_blk, hp * hd), lambda b, g, i: (b, g)),
        ],
        out_specs=pl.BlockSpec((blk, hp * hd), lambda b, g, i: (b * n_blk + i, g)),
        scratch_shapes=[
            pltpu.VMEM((s_len, hd), BF16),
            pltpu.VMEM((hp, blk, blk), F32),
            pltpu.VMEM((hp, blk, blk), F32),
            pltpu.VMEM((hp, LANES, blk), F32),
        ],
        compiler_params=_params("parallel", "parallel", "arbitrary"),
        name="moba_prompt",
    )(q, k_bf, v_bf, kmeans)


def _page_sums_kernel(pt_ref, *refs):
    del pt_ref
    page_refs, o_ref = refs[:-1], refs[-1]
    sums = [jnp.sum(r[...], axis=0) for r in page_refs]
    for i in range(len(sums) // PAGES_PER_BLOCK):
        o_ref[i] = sums[2 * i] + sums[2 * i + 1]


def page_block_sums(cache_k, layer, page_table_flat, n_seq, n_pages):
    _, _, page, n_heads, hd = cache_k.shape
    pps = PAGES_PER_STEP
    assert n_pages % pps == 0
    n_blk = n_pages // PAGES_PER_BLOCK

    def page_spec(r):
        return pl.BlockSpec((None, None, page, n_heads, hd),
                            lambda b, g, pt: (layer, pt[b * n_pages + g * pps + r], 0, 0, 0))

    return pl.pallas_call(
        _page_sums_kernel,
        out_shape=jax.ShapeDtypeStruct((n_seq, n_blk, n_heads, hd), F32),
        grid_spec=pltpu.PrefetchScalarGridSpec(
            num_scalar_prefetch=1,
            grid=(n_seq, n_pages // pps),
            in_specs=[page_spec(r) for r in range(pps)],
            out_specs=pl.BlockSpec((None, pps // PAGES_PER_BLOCK, n_heads, hd),
                                   lambda b, g, pt: (b, g, 0, 0)),
        ),
        compiler_params=_params("parallel", "arbitrary"),
        name="page_block_sums",
    )(page_table_flat, *([cache_k] * pps))


def _block_choice_kernel(bs_ref, q_ref, o_ref):
    bs = bs_ref[...]
    gate = jnp.broadcast_to(jnp.sum(bs * q_ref[...], axis=2, keepdims=True), bs.shape)
    blk_id = lax.broadcasted_iota(jnp.int32, bs.shape, 0).astype(F32)
    o_ref[...] = jnp.zeros_like(o_ref)
    for s in range(MOBA_TOPK):
        mx = jnp.max(gate, axis=0, keepdims=True)
        idx = jnp.min(jnp.where(gate == mx, blk_id, 1e9), axis=0, keepdims=True)
        o_ref[s] = idx[0].astype(jnp.int32)
        gate = jnp.where(blk_id == idx, -jnp.inf, gate)


def block_choice(blk_sums, q_s):
    n_seq, n_blk, n_heads, hd = blk_sums.shape
    return pl.pallas_call(
        _block_choice_kernel,
        out_shape=jax.ShapeDtypeStruct((n_seq, 8, n_heads, hd), jnp.int32),
        grid=(n_seq,),
        in_specs=[
            pl.BlockSpec((None, n_blk, n_heads, hd), lambda b: (b, 0, 0, 0)),
            pl.BlockSpec((None, n_heads, hd), lambda b: (b, 0, 0)),
        ],
        out_specs=pl.BlockSpec((None, 8, n_heads, hd), lambda b: (b, 0, 0, 0)),
        compiler_params=_params("parallel"),
        name="block_choice",
    )(blk_sums, q_s.reshape(n_seq, n_heads, hd))


def _sample_attend_kernel(phys_ref, idx_ref, q_ref, kn_ref, vn_ref, ck_ref, cv_ref, o_ref,
                          kbuf, vbuf, sem, *, n_heads, past_len, layer):
    n_sel = MOBA_TOPK * PAGES_PER_BLOCK
    n_keys = n_sel * PAGE_SIZE
    b = pl.program_id(0)
    nb = pl.num_programs(0)
    hd = HEAD_DIM
    scale = hd ** -0.5
    slot = lax.rem(b, 2)

    def copies(seq, slot_):
        out = []
        for h in range(n_heads):
            for s in range(n_sel):
                page = phys_ref[(seq * n_heads + h) * n_sel + s]
                rows = pl.ds(s * PAGE_SIZE, PAGE_SIZE)
                out.append(pltpu.make_async_copy(ck_ref.at[layer, page, :, h, :],
                                                 kbuf.at[slot_, h, rows, :], sem.at[0, slot_]))
                out.append(pltpu.make_async_copy(cv_ref.at[layer, page, :, h, :],
                                                 vbuf.at[slot_, h, rows, :], sem.at[1, slot_]))
        return out

    @pl.when(b == 0)
    def _():
        for cp in copies(b, slot):
            cp.start()

    @pl.when(b + 1 < nb)
    def _():
        for cp in copies(b + 1, 1 - slot):
            cp.start()

    for cp in copies(b, slot):
        cp.wait()

    key_row = lax.broadcasted_iota(jnp.int32, (PAGE_SIZE, hd), 0)
    for h in range(n_heads):
        slope = 2.0 ** (-8.0 * (h + 1) / n_heads)
        q = q_ref[h:h + 1, :]
        qk = lax.dot_general(kbuf[slot, h], jnp.broadcast_to(q, (hd, hd)), (((1,), (1,)), ((), ())),
                             preferred_element_type=F32, precision=lax.Precision.HIGHEST)
        dists = []
        for s in range(n_sel):
            blk = idx_ref[(b * n_heads + h) * MOBA_TOPK + s // PAGES_PER_BLOCK]
            pos0 = blk * MOBA_BLOCK + (s % PAGES_PER_BLOCK) * PAGE_SIZE
            dists.append((past_len - pos0 - key_row).astype(F32))
        sc = qk * scale - slope * jnp.concatenate(dists, axis=0)
        s_self = jnp.sum(q * kn_ref[h:h + 1, :], axis=1, keepdims=True) * scale
        m = jnp.maximum(jnp.max(sc, axis=0, keepdims=True), s_self)
        p = jnp.exp(sc - m)
        p_self = jnp.exp(s_self - m)
        l = jnp.sum(p, axis=0, keepdims=True) + p_self
        acc = jnp.sum(p * vbuf[slot, h], axis=0, keepdims=True) + p_self * vn_ref[h:h + 1, :]
        o_ref[h:h + 1, :] = acc / l


def sample_attend(q_s, k_s, v_s, cache_k, cache_v, layer, phys, idx_flat, past_len):
    n_seq, width = q_s.shape
    hd = HEAD_DIM
    n_heads = width // hd
    n_keys = MOBA_TOPK * MOBA_BLOCK
    row_spec = pl.BlockSpec((None, n_heads, hd), lambda b, ph, ix: (b, 0, 0))
    to3 = lambda a: a.reshape(n_seq, n_heads, hd)
    out = pl.pallas_call(
        functools.partial(_sample_attend_kernel, n_heads=n_heads, past_len=past_len, layer=layer),
        out_shape=jax.ShapeDtypeStruct((n_seq, n_heads, hd), F32),
        grid_spec=pltpu.PrefetchScalarGridSpec(
            num_scalar_prefetch=2,
            grid=(n_seq,),
            in_specs=[row_spec, row_spec, row_spec,
                      pl.BlockSpec(memory_space=pl.ANY), pl.BlockSpec(memory_space=pl.ANY)],
            out_specs=row_spec,
            scratch_shapes=[
                pltpu.VMEM((2, n_heads, n_keys, hd), F32),
                pltpu.VMEM((2, n_heads, n_keys, hd), F32),
                pltpu.SemaphoreType.DMA((2, 2)),
            ],
        ),
        compiler_params=_params("arbitrary"),
        name="sample_attend",
    )(phys, idx_flat, to3(q_s), to3(k_s), to3(v_s), cache_k, cache_v)
    return out.reshape(n_seq, width)


def _sgu_merge_kernel(x_ref, att_ref, u_ref, vg_ref, ga_ref, gb_ref, gs_ref, ws_ref, bs_ref,
                      pa_ref, pb_ref, wo_ref, *refs, chunked):
    if chunked:
        o_ref, m_ref, acc_ref = refs
    else:
        o_ref, vn_out_ref, m_ref, acc_ref = refs
    j = pl.program_id(1)
    tm = x_ref.shape[0]

    @pl.when(j == 0)
    def _():
        vn = _rms_rows(vg_ref[...].astype(F32), gs_ref[...])
        if chunked:
            ch = SGU_CHUNK
            gd = SGU_GROUP_DIM
            r = lax.broadcasted_iota(jnp.int32, (ch, ch), 0)
            c = lax.broadcasted_iota(jnp.int32, (ch, ch), 1)
            vnb = vn.astype(BF16)
            for g in range(ws_ref.shape[0]):
                w = jnp.where(c <= r, ws_ref[g], 0.0).astype(BF16)
                bias = bs_ref[g]
                for cc in range(tm // ch):
                    rows = slice(cc * ch, (cc + 1) * ch)
                    cols = slice(g * gd, (g + 1) * gd)
                    z = jnp.dot(w, vnb[rows, cols], preferred_element_type=F32) + bias
                    m_ref[rows, cols] = (u_ref[rows, cols].astype(F32) * z).astype(m_ref.dtype)
        else:
            vn_out_ref[...] = vn
            z = vn * ws_ref[...] + bs_ref[...]
            m_ref[...] = (u_ref[...].astype(F32) * z).astype(m_ref.dtype)

    pa = jnp.dot(att_ref[...].astype(BF16), pa_ref[...], preferred_element_type=F32)
    pb = jnp.dot(m_ref[...], pb_ref[...], preferred_element_type=F32)
    mixed = _sigmoid(ga_ref[...].astype(F32)) * pa + _sigmoid(gb_ref[...].astype(F32)) * pb
    part = jnp.dot(mixed.astype(BF16), wo_ref[...], preferred_element_type=F32)

    @pl.when(j == 0)
    def _():
        acc_ref[...] = part

    @pl.when(j != 0)
    def _():
        acc_ref[...] += part

    @pl.when(j == pl.num_programs(1) - 1)
    def _():
        o_ref[...] = x_ref[...] + acc_ref[...]


def sgu_merge(x, att, rest, g_sgu, w_sgu, b_sgu, w_pa, w_pb, w_o, chunked, tm, tn):
    t, d = x.shape
    a_w = att.shape[1]
    b_w = g_sgu.shape[0]
    tm = min(tm, t)
    assert d % tn == 0 and b_w % tn == 0 and a_w == b_w and tn == b_w
    nj = d // tn
    n_groups = w_sgu.shape[0]
    if chunked:
        assert tm % SGU_CHUNK == 0
        ws = w_sgu
        bs = jnp.broadcast_to(b_sgu[:, :, None], (n_groups, SGU_CHUNK, SGU_GROUP_DIM))
        ws_spec = pl.BlockSpec((n_groups, SGU_CHUNK, SGU_CHUNK), lambda i, j: (0, 0, 0))
        bs_spec = pl.BlockSpec((n_groups, SGU_CHUNK, SGU_GROUP_DIM), lambda i, j: (0, 0, 0))
    else:
        ws = jnp.repeat(w_sgu[:, 0, 0], SGU_GROUP_DIM).reshape(1, b_w)
        bs = jnp.repeat(b_sgu[:, 0], SGU_GROUP_DIM).reshape(1, b_w)
        ws_spec = bs_spec = pl.BlockSpec((1, b_w), lambda i, j: (0, 0))
    u_blk = b_w // tn
    in_specs = [
        pl.BlockSpec((tm, d), lambda i, j: (i, 0)),
        pl.BlockSpec((tm, a_w), lambda i, j: (i, 0)),
        pl.BlockSpec((tm, b_w), lambda i, j: (i, 0)),
        pl.BlockSpec((tm, b_w), lambda i, j: (i, 1)),
        pl.BlockSpec((tm, tn), lambda i, j: (i, 2 * u_blk + j)),
        pl.BlockSpec((tm, tn), lambda i, j: (i, 2 * u_blk + nj + j)),
        pl.BlockSpec((1, b_w), lambda i, j: (0, 0)),
        ws_spec, bs_spec,
        pl.BlockSpec((a_w, tn), lambda i, j: (0, j)),
        pl.BlockSpec((b_w, tn), lambda i, j: (0, j)),
        pl.BlockSpec((tn, d), lambda i, j: (j, 0)),
    ]
    x_spec = pl.BlockSpec((tm, d), lambda i, j: (i, 0))
    if chunked:
        out_shape = jax.ShapeDtypeStruct((t, d), F32)
        out_specs = x_spec
    else:
        out_shape = (jax.ShapeDtypeStruct((t, d), F32), jax.ShapeDtypeStruct((t, b_w), F32))
        out_specs = (x_spec, pl.BlockSpec((tm, b_w), lambda i, j: (i, 0)))
    return pl.pallas_call(
        functools.partial(_sgu_merge_kernel, chunked=chunked),
        out_shape=out_shape,
        grid=(t // tm, nj),
        in_specs=in_specs,
        out_specs=out_specs,
        scratch_shapes=[pltpu.VMEM((tm, b_w), BF16), pltpu.VMEM((tm, d), F32)],
        compiler_params=_params("parallel", "arbitrary"),
        name="sgu_merge",
    )(x, att, rest, rest, rest, rest, g_sgu.reshape(1, b_w), ws, bs, w_pa, w_pb, w_o)


def _moe_route_kernel(x_ref, g_ref, wr_ref, br_ref, xn_ref, comb_ref, *, as_list):
    xn = _rms_rows(x_ref[...], g_ref[...])
    xn_ref[...] = xn.astype(xn_ref.dtype)
    logits = jnp.dot(xn, wr_ref[...], preferred_element_type=F32,
                     precision=lax.Precision.HIGHEST) + br_ref[...]
    lane = lax.broadcasted_iota(jnp.int32, logits.shape, 1)
    lanef = lane.astype(F32)
    e0 = ROUTER_EXPERT_LANE0
    is_grp = lane < e0
    gl = jnp.where(is_grp, logits, -jnp.inf)
    gmax = jnp.max(gl, axis=1, keepdims=True)
    grp = jnp.min(jnp.where(gl == gmax, lanef, 1e9), axis=1, keepdims=True)
    p_grp = 1.0 / jnp.sum(jnp.where(is_grp, jnp.exp(logits - gmax), 0.0), axis=1, keepdims=True)
    lane_grp = jnp.right_shift(lane - e0, 2).astype(F32)
    in_grp = (lane >= e0) & (lane < e0 + N_EXPERTS) & (lane_grp == grp)
    el = jnp.where(in_grp, logits, -jnp.inf)
    t1 = jnp.max(el, axis=1, keepdims=True)
    i1 = jnp.min(jnp.where(el == t1, lanef, 1e9), axis=1, keepdims=True)
    el2 = jnp.where(lanef == i1, -jnp.inf, el)
    t2 = jnp.max(el2, axis=1, keepdims=True)
    i2 = jnp.min(jnp.where(el2 == t2, lanef, 1e9), axis=1, keepdims=True)
    e2 = jnp.exp(t2 - t1)
    w1 = (1.0 / (1.0 + e2)) * p_grp
    w2 = (e2 / (1.0 + e2)) * p_grp
    if as_list:
        comb_ref[...] = jnp.where(lane == 0, i1 - e0, jnp.where(lane == 1, i2 - e0,
                                  jnp.where(lane == 2, w1, jnp.where(lane == 3, w2, 0.0))))
    else:
        comb_ref[...] = jnp.where(lanef == i1, w1, jnp.where(lanef == i2, w2, 0.0))


def moe_route(x, g_ffn, w_rg, b_rg, w_re, b_re, tm, as_list):
    t, d = x.shape
    tm = min(tm, t)
    assert EXP_PER_GROUP == 4 and w_rg.shape[1] == N_EXP_GROUPS and w_re.shape[1] == N_EXPERTS
    pad = LANES - N_EXP_GROUPS - N_EXPERTS
    wr = jnp.concatenate([w_rg, w_re, jnp.zeros((d, pad), F32)], axis=1)
    br = jnp.concatenate([b_rg, b_re, jnp.zeros((pad,), F32)]).reshape(1, LANES)
    return pl.pallas_call(
        functools.partial(_moe_route_kernel, as_list=as_list),
        out_shape=(jax.ShapeDtypeStruct((t, d), F32 if as_list else BF16),
                   jax.ShapeDtypeStruct((t, LANES), F32)),
        grid=(t // tm,),
        in_specs=[
            pl.BlockSpec((tm, d), lambda i: (i, 0)),
            pl.BlockSpec((1, d), lambda i: (0, 0)),
            pl.BlockSpec((d, LANES), lambda i: (0, 0)),
            pl.BlockSpec((1, LANES), lambda i: (0, 0)),
        ],
        out_specs=(pl.BlockSpec((tm, d), lambda i: (i, 0)), pl.BlockSpec((tm, LANES), lambda i: (i, 0))),
        compiler_params=_params("parallel"),
        name="moe_route",
    )(x, g_ffn.reshape(1, d), wr, br)


def _moe_experts_kernel(xn_ref, comb_ref, x_ref, wg_ref, wu_ref, wd_ref, gf_ref, o_ref, acc_ref):
    e = pl.program_id(1)
    lane = lax.broadcasted_iota(jnp.int32, comb_ref.shape, 1)
    c = jnp.sum(jnp.where(lane == e + ROUTER_EXPERT_LANE0, comb_ref[...], 0.0), axis=1, keepdims=True)
    xn = xn_ref[...]
    g = jnp.dot(xn, wg_ref[...].astype(BF16), preferred_element_type=F32)
    u = jnp.dot(xn, wu_ref[...].astype(BF16), preferred_element_type=F32)
    h = (g * _sigmoid(g)) * u
    part = jnp.dot((h * c).astype(BF16), wd_ref[...].astype(BF16), preferred_element_type=F32)

    @pl.when(e == 0)
    def _():
        acc_ref[...] = part

    @pl.when(e != 0)
    def _():
        acc_ref[...] += part

    @pl.when(e == pl.num_programs(1) - 1)
    def _():
        o_ref[...] = _rms_rows(x_ref[...] + acc_ref[...], gf_ref[...])


def moe_experts(xn, comb, x, w_gate, w_up, w_down, g_final, tm):
    t, d = x.shape
    n_exp, _, f = w_gate.shape
    tm = min(tm, t)
    return pl.pallas_call(
        _moe_experts_kernel,
        out_shape=jax.ShapeDtypeStruct((t, d), F32),
        grid=(t // tm, n_exp),
        in_specs=[
            pl.BlockSpec((tm, d), lambda i, e: (i, 0)),
            pl.BlockSpec((tm, LANES), lambda i, e: (i, 0)),
            pl.BlockSpec((tm, d), lambda i, e: (i, 0)),
            pl.BlockSpec((None, d, f), lambda i, e: (e, 0, 0)),
            pl.BlockSpec((None, d, f), lambda i, e: (e, 0, 0)),
            pl.BlockSpec((None, f, d), lambda i, e: (e, 0, 0)),
            pl.BlockSpec((1, d), lambda i, e: (0, 0)),
        ],
        out_specs=pl.BlockSpec((tm, d), lambda i, e: (i, 0)),
        scratch_shapes=[pltpu.VMEM((tm, d), F32)],
        compiler_params=_params("parallel", "arbitrary"),
        name="moe_experts",
    )(xn, comb, x, w_gate, w_up, w_down, g_final.reshape(1, d))


def _dispatch_plan(e_tok, tm):
    n_pairs = e_tok.size
    n_exp = N_EXPERTS
    i32 = jnp.int32
    ef = e_tok.reshape(-1)
    order = jnp.argsort(ef, stable=True).astype(i32)
    inv = jnp.argsort(order).astype(i32)
    sorted_e = ef[order]
    experts = jnp.arange(n_exp, dtype=i32)
    c_start = jnp.sum((sorted_e[None, :] < experts[:, None]).astype(i32), axis=1)
    counts = jnp.concatenate([c_start[1:], jnp.full((1,), n_pairs, i32)]) - c_start
    padded = ((counts + tm - 1) // tm) * tm
    g_end = jnp.cumsum(padded).astype(i32)
    g_start = g_end - padded
    shift = g_start - c_start
    pair_row = inv + shift[ef]
    n_rows = n_pairs + n_exp * tm
    n_tiles = n_rows // tm
    tile_start = jnp.arange(n_tiles, dtype=i32) * tm
    tile_expert = jnp.minimum(jnp.sum((g_end[None, :] <= tile_start[:, None]).astype(i32), axis=1),
                              n_exp - 1)
    row = jnp.arange(n_rows, dtype=i32)
    s_pos = row - jnp.repeat(shift[tile_expert], tm)
    row_token = order[jnp.clip(s_pos, 0, n_pairs - 1)] // 2
    return row_token.astype(i32), tile_expert.astype(i32), pair_row.astype(i32)


def _row_gather(src_hbm, dst, sem, row_ids, base, n_rows, slot):
    def body(r, carry):
        tok = row_ids[base + r]
        pltpu.make_async_copy(src_hbm.at[pl.ds(tok, 1), :], dst.at[slot, pl.ds(r, 1), :], sem.at[slot]).start()
        return carry
    lax.fori_loop(0, n_rows, body, 0, unroll=8)


def _row_gather_wait(src_hbm, dst, sem, n_rows, slot):
    def body(r, carry):
        pltpu.make_async_copy(src_hbm.at[pl.ds(0, 1), :], dst.at[slot, pl.ds(r, 1), :], sem.at[slot]).wait()
        return carry
    lax.fori_loop(0, n_rows, body, 0, unroll=8)


def _moe_sparse_kernel(tok_ref, texp_ref, xn_hbm, wg_ref, wu_ref, wd_ref, o_ref, xbuf, wgb, wub, wdb, sem):
    i = pl.program_id(0)
    n = pl.num_programs(0)
    tm = o_ref.shape[0]
    slot = lax.rem(i, 2)

    @pl.when(i == 0)
    def _():
        _row_gather(xn_hbm, xbuf, sem, tok_ref, 0, tm, 0)

    @pl.when((i == 0) | (texp_ref[i] != texp_ref[jnp.maximum(i - 1, 0)]))
    def _():
        wgb[...] = wg_ref[...].astype(BF16)
        wub[...] = wu_ref[...].astype(BF16)
        wdb[...] = wd_ref[...].astype(BF16)

    _row_gather_wait(xn_hbm, xbuf, sem, tm, slot)
    xs = xbuf[slot].astype(BF16)
    nxt = jnp.minimum(i + 1, n - 1)
    f = wg_ref.shape[1]
    d = wd_ref.shape[1]
    nw = MOE_MATMUL_PIECE
    n_pieces = 2 * (f // nw) + d // nw
    per_piece = -(-tm // n_pieces)

    def start_rows(piece):
        for r in range(piece * per_piece, min((piece + 1) * per_piece, tm)):
            tok = tok_ref[nxt * tm + r]
            pltpu.make_async_copy(xn_hbm.at[pl.ds(tok, 1), :], xbuf.at[1 - slot, pl.ds(r, 1), :],
                                  sem.at[1 - slot]).start()

    piece = 0
    gs, us = [], []
    for c in range(f // nw):
        start_rows(piece)
        gs.append(jnp.dot(xs, wgb[:, c * nw:(c + 1) * nw], preferred_element_type=F32))
        piece += 1
    for c in range(f // nw):
        start_rows(piece)
        us.append(jnp.dot(xs, wub[:, c * nw:(c + 1) * nw], preferred_element_type=F32))
        piece += 1
    g = jnp.concatenate(gs, axis=1)
    u = jnp.concatenate(us, axis=1)
    hb = ((g * _sigmoid(g)) * u).astype(BF16)
    for c in range(d // nw):
        start_rows(piece)
        o_ref[:, c * nw:(c + 1) * nw] = jnp.dot(hb, wdb[:, c * nw:(c + 1) * nw],
                                                preferred_element_type=F32)
        piece += 1

    @pl.when(i == n - 1)
    def _():
        _row_gather_wait(xn_hbm, xbuf, sem, tm, 1 - slot)


def moe_sparse_experts(xn, row_token, tile_expert, w_gate, w_up, w_down, tm):
    t, d = xn.shape
    n_exp, _, f = w_gate.shape
    n_rows = row_token.shape[0]
    return pl.pallas_call(
        _moe_sparse_kernel,
        out_shape=jax.ShapeDtypeStruct((n_rows, d), F32),
        grid_spec=pltpu.PrefetchScalarGridSpec(
            num_scalar_prefetch=2,
            grid=(n_rows // tm,),
            in_specs=[
                pl.BlockSpec(memory_space=pl.ANY),
                pl.BlockSpec((None, d, f), lambda i, tok, te: (te[i], 0, 0)),
                pl.BlockSpec((None, d, f), lambda i, tok, te: (te[i], 0, 0)),
                pl.BlockSpec((None, f, d), lambda i, tok, te: (te[i], 0, 0)),
            ],
            out_specs=pl.BlockSpec((tm, d), lambda i, tok, te: (i, 0)),
            scratch_shapes=[pltpu.VMEM((2, tm, d), F32), pltpu.VMEM((d, f), BF16), pltpu.VMEM((d, f), BF16),
                            pltpu.VMEM((f, d), BF16), pltpu.SemaphoreType.DMA((2,))],
        ),
        compiler_params=_params("arbitrary"),
        name="moe_sparse_experts",
    )(row_token, tile_expert, xn, w_gate, w_up, w_down)


def _moe_combine_kernel(row_ref, ys_hbm, x_ref, route_ref, gf_ref, o_ref, ybuf, sem):
    i = pl.program_id(0)
    n = pl.num_programs(0)
    tm = o_ref.shape[0]
    slot = lax.rem(i, 2)

    @pl.when(i == 0)
    def _():
        _row_gather(ys_hbm, ybuf, sem, row_ref, 0, 2 * tm, 0)

    nxt = jnp.minimum(i + 1, n - 1)
    for r in range(2 * tm):
        row = row_ref[nxt * 2 * tm + r]
        pltpu.make_async_copy(ys_hbm.at[pl.ds(row, 1), :], ybuf.at[1 - slot, pl.ds(r, 1), :],
                              sem.at[1 - slot]).start()

    _row_gather_wait(ys_hbm, ybuf, sem, 2 * tm, slot)
    k = TOP_K_IN_GROUP
    route = route_ref[...]
    lane = lax.broadcasted_iota(jnp.int32, route.shape, 1)
    w1 = jnp.sum(jnp.where(lane == k, route, 0.0), axis=1, keepdims=True)
    w2 = jnp.sum(jnp.where(lane == k + 1, route, 0.0), axis=1, keepdims=True)
    y = w1 * ybuf[slot, 0:tm, :] + w2 * ybuf[slot, tm:2 * tm, :]
    o_ref[...] = _rms_rows(x_ref[...] + y, gf_ref[...])

    @pl.when(i == n - 1)
    def _():
        _row_gather_wait(ys_hbm, ybuf, sem, 2 * tm, 1 - slot)


def moe_combine(ys, pair_row, x, route, g_final, tm):
    t, d = x.shape
    rows = jnp.transpose(pair_row.reshape(t // tm, tm, 2), (0, 2, 1)).reshape(-1)
    return pl.pallas_call(
        _moe_combine_kernel,
        out_shape=jax.ShapeDtypeStruct((t, d), F32),
        grid_spec=pltpu.PrefetchScalarGridSpec(
            num_scalar_prefetch=1,
            grid=(t // tm,),
            in_specs=[
                pl.BlockSpec(memory_space=pl.ANY),
                pl.BlockSpec((tm, d), lambda i, rw: (i, 0)),
                pl.BlockSpec((tm, LANES), lambda i, rw: (i, 0)),
                pl.BlockSpec((1, d), lambda i, rw: (0, 0)),
            ],
            out_specs=pl.BlockSpec((tm, d), lambda i, rw: (i, 0)),
            scratch_shapes=[pltpu.VMEM((2, 2 * tm, d), F32), pltpu.SemaphoreType.DMA((2,))],
        ),
        compiler_params=_params("arbitrary"),
        name="moe_combine",
    )(rows, ys, x, route, g_final.reshape(1, d))


def kernel(x_prompt, x_sample, cache_k, cache_v, page_table, g_mix, w_in, g_sgu, w_sgu, b_sgu, w_proj_a, w_proj_b, w_out, g_ffn, w_router_group, b_router_group, w_router_expert, b_router_expert, w_exp_gate, w_exp_up, w_exp_down, g_final):
    depth = g_mix.shape[0]
    assert depth == 1
    layer = 0
    b_sz, s_len, d = x_prompt.shape
    n_seq, n_new, _ = x_sample.shape
    assert n_new == 1
    _, n_pool, page, n_heads, hd = cache_k.shape
    assert page == PAGE_SIZE and hd == HEAD_DIM
    a_w = n_heads * hd
    b_w = g_sgu.shape[1]
    n_pages = page_table.shape[1]
    past_len = n_pages * PAGE_SIZE
    assert past_len % MOBA_BLOCK == 0 and n_pages // PAGES_PER_BLOCK >= MOBA_TOPK

    w_in_l = w_in[layer]
    w_qkv = w_in_l[:, 0:3 * a_w].astype(BF16)
    w_q, w_k, w_v = (w_qkv[:, c * a_w:(c + 1) * a_w] for c in range(3))
    w_rest = w_in_l[:, 3 * a_w:].astype(BF16)
    w_pa = w_proj_a[layer].astype(BF16)
    w_pb = w_proj_b[layer].astype(BF16)
    w_o = w_out[layer].astype(BF16)
    w_eg, w_eu, w_ed = w_exp_gate[layer], w_exp_up[layer], w_exp_down[layer]

    def dense_tail(x2, att, rest, chunked):
        merged = sgu_merge(x2, att, rest, g_sgu[layer], w_sgu[layer], b_sgu[layer], w_pa, w_pb, w_o,
                           chunked=chunked, tm=512, tn=1024)
        x1, vn = (merged, None) if chunked else merged
        sparse = x1.shape[0] >= MOE_SPARSE_MIN_TOKENS
        xn, route = moe_route(x1, g_ffn[layer], w_router_group[layer], b_router_group[layer],
                              w_router_expert[layer], b_router_expert[layer], tm=256, as_list=sparse)
        if sparse:
            e_tok = route[:, 0:TOP_K_IN_GROUP].astype(jnp.int32)
            row_token, tile_expert, pair_row = _dispatch_plan(e_tok, MOE_ROW_TILE)
            ys = moe_sparse_experts(xn, row_token, tile_expert, w_eg, w_eu, w_ed, tm=MOE_ROW_TILE)
            y = moe_combine(ys, pair_row.reshape(-1, TOP_K_IN_GROUP), x1, route, g_final, tm=MOE_ROW_TILE)
        else:
            y = moe_experts(xn, route, x1, w_eg, w_eu, w_ed, g_final, tm=512)
        return y, vn

    xp = x_prompt.reshape(b_sz * s_len, d)
    g_in = g_mix[layer]
    q_p, k_p, v_p, kb_p, vb_p = norm_qkv(xp, g_in, w_qkv, tm=512)
    (rest_p,) = norm_matmul(xp, g_in, w_rest, (BF16,), tm=1024, tn=1024)
    att_p = moba_prompt(q_p, kb_p, vb_p, block_means(k_p), b_sz, s_len, BF16)
    y_p, _ = dense_tail(xp, att_p, rest_p, chunked=True)

    xs = x_sample.reshape(n_seq, d)
    (q_s,) = norm_matmul(xs, g_in, w_q, (F32,), tm=n_seq, tn=1024)
    (k_s,) = norm_matmul(xs, g_in, w_k, (F32,), tm=n_seq, tn=1024)
    (v_s,) = norm_matmul(xs, g_in, w_v, (F32,), tm=n_seq, tn=1024)
    (rest_s,) = norm_matmul(xs, g_in, w_rest, (F32,), tm=n_seq, tn=1024)
    pt_flat = page_table.reshape(-1)
    blk_sums = page_block_sums(cache_k, layer, pt_flat, n_seq, n_pages)
    picks = block_choice(blk_sums, q_s)
    idx = jnp.transpose(picks[:, :MOBA_TOPK, :, 0], (0, 2, 1))
    pages = idx[..., None] * PAGES_PER_BLOCK + jnp.arange(PAGES_PER_BLOCK, dtype=jnp.int32)
    phys = jnp.take_along_axis(page_table[:, None, :], pages.reshape(n_seq, n_heads, -1), axis=2)
    att_s = sample_attend(q_s, k_s, v_s, cache_k, cache_v, layer, phys.reshape(-1), idx.reshape(-1),
                          past_len)
    y_s, vn_s = dense_tail(xs, att_s, rest_s, chunked=False)

    return (
        y_p.reshape(b_sz, s_len, d),
        y_s.reshape(n_seq, n_new, d),
        k_p.reshape(depth, b_sz, s_len, n_heads, hd),
        v_p.reshape(depth, b_sz, s_len, n_heads, hd),
        k_s.reshape(depth, n_seq, n_new, n_heads, hd),
        v_s.reshape(depth, n_seq, n_new, n_heads, hd),
        vn_s.reshape(depth, n_seq, n_new, b_w),
    )
```

```python
import functools

import jax
import jax.numpy as jnp
from jax import lax
from jax.experimental import pallas as pl
from jax.experimental.pallas import tpu as pltpu

NORM_EPS = 1e-6
NEG_INF = -1e30
HEAD_DIM = 128
MOBA_BLOCK = 256
MOBA_TOPK = 3
PAGE_SIZE = 128
PAGES_PER_BLOCK = MOBA_BLOCK // PAGE_SIZE
SGU_CHUNK = 128
SGU_GROUP_DIM = 128
N_EXP_GROUPS = 4
EXP_PER_GROUP = 4
N_EXPERTS = N_EXP_GROUPS * EXP_PER_GROUP
TOP_K_IN_GROUP = 2
MOE_ROW_TILE = 256
MOE_SPARSE_MIN_TOKENS = 2048
GATHER_SLOTS = 3
LANES = 128
ROUTER_EXPERT_LANE0 = N_EXP_GROUPS
VMEM_LIMIT_BYTES = 56 * 1024 * 1024
PAGE_UNIT = 8
MOBA_HEADS_PER_STEP = 8
LOG2E = 1.4426950408889634

F32 = jnp.float32
BF16 = jnp.bfloat16


def _params(*semantics):
    return pltpu.CompilerParams(dimension_semantics=semantics, vmem_limit_bytes=VMEM_LIMIT_BYTES)


def _rms_rows(x, g):
    inv = lax.rsqrt(jnp.mean(x * x, axis=-1, keepdims=True) + NORM_EPS)
    return (x * inv) * g


def _sigmoid(x):
    return 1.0 / (1.0 + jnp.exp(-x))


def _norm_matmul_kernel(x_ref, g_ref, w_ref, *refs, row_chunk):
    out_refs, xn_ref = refs[:-1], refs[-1]

    @pl.when(pl.program_id(1) == 0)
    def _():
        def body(r, carry):
            rows = pl.ds(pl.multiple_of(r * row_chunk, row_chunk), row_chunk)
            xn_ref[rows, :] = _rms_rows(x_ref[rows, :], g_ref[...]).astype(xn_ref.dtype)
            return carry
        lax.fori_loop(0, x_ref.shape[0] // row_chunk, body, 0)

    res = jnp.dot(xn_ref[...], w_ref[...], preferred_element_type=F32)
    for o_ref in out_refs:
        o_ref[...] = res.astype(o_ref.dtype)


def _norm_qkv_kernel(x_ref, g_ref, w_ref, q_ref, k_ref, v_ref, kb_ref, vb_ref, xn_ref, *, row_chunk):
    j = pl.program_id(1)

    @pl.when(j == 0)
    def _():
        def body(r, carry):
            rows = pl.ds(pl.multiple_of(r * row_chunk, row_chunk), row_chunk)
            xn_ref[rows, :] = _rms_rows(x_ref[rows, :], g_ref[...]).astype(xn_ref.dtype)
            return carry
        lax.fori_loop(0, x_ref.shape[0] // row_chunk, body, 0)

    res = jnp.dot(xn_ref[...], w_ref[...], preferred_element_type=F32)

    @pl.when(j == 0)
    def _():
        q_ref[...] = res

    @pl.when(j == 1)
    def _():
        k_ref[...] = res
        kb_ref[...] = res.astype(kb_ref.dtype)

    @pl.when(j == 2)
    def _():
        v_ref[...] = res
        vb_ref[...] = res.astype(vb_ref.dtype)


def norm_qkv(x, g, w_qkv, tm):
    t, d = x.shape
    a_w = w_qkv.shape[1] // 3
    tm = min(tm, t)
    row_chunk = min(tm, 128)
    out_spec = pl.BlockSpec((tm, a_w), lambda i, j: (i, 0))
    return pl.pallas_call(
        functools.partial(_norm_qkv_kernel, row_chunk=row_chunk),
        out_shape=tuple(jax.ShapeDtypeStruct((t, a_w), dt) for dt in (F32, F32, F32, BF16, BF16)),
        grid=(t // tm, 3),
        in_specs=[
            pl.BlockSpec((tm, d), lambda i, j: (i, 0)),
            pl.BlockSpec((1, d), lambda i, j: (0, 0)),
            pl.BlockSpec((d, a_w), lambda i, j: (0, j)),
        ],
        out_specs=tuple(out_spec for _ in range(5)),
        scratch_shapes=[pltpu.VMEM((tm, d), BF16)],
        compiler_params=_params("parallel", "arbitrary"),
        name="norm_qkv",
    )(x, g.reshape(1, d), w_qkv)


def norm_matmul(x, g, w, out_dtypes, tm, tn):
    t, d = x.shape
    n = w.shape[1]
    tm = min(tm, t)
    tn = min(tn, n)
    row_chunk = min(tm, 128)
    out_spec = pl.BlockSpec((tm, tn), lambda i, j: (i, j))
    return pl.pallas_call(
        functools.partial(_norm_matmul_kernel, row_chunk=row_chunk),
        out_shape=tuple(jax.ShapeDtypeStruct((t, n), dt) for dt in out_dtypes),
        grid=(t // tm, n // tn),
        in_specs=[
            pl.BlockSpec((tm, d), lambda i, j: (i, 0)),
            pl.BlockSpec((1, d), lambda i, j: (0, 0)),
            pl.BlockSpec((d, tn), lambda i, j: (0, j)),
        ],
        out_specs=tuple(out_spec for _ in out_dtypes),
        scratch_shapes=[pltpu.VMEM((tm, d), BF16)],
        compiler_params=_params("parallel", "arbitrary"),
        name="norm_matmul",
    )(x, g.reshape(1, d), w)


def _block_means_kernel(k_ref, o_ref):
    blk = MOBA_BLOCK
    for n in range(o_ref.shape[0]):
        o_ref[n:n + 1, :] = jnp.sum(k_ref[n * blk:(n + 1) * blk, :], axis=0, keepdims=True) * (1.0 / blk)


def block_means(k):
    t, width = k.shape
    per_step = 8
    rows = per_step * MOBA_BLOCK
    assert t % rows == 0
    return pl.pallas_call(
        _block_means_kernel,
        out_shape=jax.ShapeDtypeStruct((t // MOBA_BLOCK, width), F32),
        grid=(t // rows,),
        in_specs=[pl.BlockSpec((rows, width), lambda i: (i, 0))],
        out_specs=pl.BlockSpec((per_step, width), lambda i: (i, 0)),
        compiler_params=_params("parallel"),
        name="block_means",
    )(k)


def _moba_prompt_kernel(pt_ref, q_ref, k_ref, v_ref, kmean_ref, ck_hbm, o_ref, sums_hbm,
                        e_ref, bias_ref, biasd_ref, gate_ref, pbuf, stage, psem, osem,
                        *, n_blk, n_heads, hp, layer):
    blk = MOBA_BLOCK
    hd = HEAD_DIM
    bat = pl.program_id(0)
    grp = pl.program_id(1)
    qi = pl.program_id(2)
    n_grp = pl.num_programs(1)
    c2 = hd ** -0.5 * LOG2E
    nt = (((1,), (1,)), ((), ()))
    slopes = [jnp.exp2(jnp.full((1, 1), -8.0 / n_heads, F32) * (grp * hp + t + 1).astype(F32)) * LOG2E
              for t in range(hp)]

    pu = PAGE_UNIT
    units_used = pt_ref.shape[0] // pu
    sums_per_unit = pu // PAGES_PER_BLOCK
    unit0 = (bat * n_grp + grp) * (n_blk * (n_blk + 1) // 2) + (qi * (qi + 1)) // 2

    def page_copies(u, slot):
        return [pltpu.make_async_copy(ck_hbm.at[layer, pt_ref[u * pu + r]], pbuf.at[slot, r], psem.at[slot])
                for r in range(pu)]

    def sums_copy(u, slot):
        return pltpu.make_async_copy(stage.at[slot], sums_hbm.at[pl.ds(u * sums_per_unit, sums_per_unit)],
                                     osem.at[slot])

    def page_unit(u):
        slot = lax.rem(u, 2)

        @pl.when((u >= 2) & (u - 2 < units_used))
        def _():
            sums_copy(u - 2, slot).wait()

        @pl.when(u < units_used)
        def _():
            @pl.when(u == 0)
            def _():
                for cp in page_copies(u, slot):
                    cp.start()

            @pl.when(u + 1 < units_used)
            def _():
                for cp in page_copies(u + 1, 1 - slot):
                    cp.start()

            for cp in page_copies(u, slot):
                cp.wait()
            for i in range(sums_per_unit):
                stage[slot, i] = sum(jnp.sum(pbuf[slot, PAGES_PER_BLOCK * i + r], axis=0)
                                     for r in range(PAGES_PER_BLOCK))
            sums_copy(u, slot).start()

    page_unit(unit0)

    @pl.when(qi == 0)
    def _():
        lane = lax.broadcasted_iota(jnp.int32, (blk, hd), 1)
        for n in range(n_blk):
            e_ref[n * blk:(n + 1) * blk, :] = jnp.where(lane == n, 1.0, 0.0).astype(BF16)
        r = lax.broadcasted_iota(jnp.int32, (blk, blk), 0)
        c = lax.broadcasted_iota(jnp.int32, (blk, blk), 1)
        for t in range(hp):
            alibi = -slopes[t] * (r - c).astype(F32)
            bias_ref[t] = alibi
            biasd_ref[t] = jnp.where(c <= r, alibi, NEG_INF)

    own = pl.ds(pl.multiple_of(qi * blk, blk), blk)
    q_augs, carry0 = [], []
    for t in range(hp):
        cols = slice(t * hd, (t + 1) * hd)
        q = q_ref[:, cols]
        qb = q.astype(BF16)
        gate_ref[t, 0:n_blk, :] = lax.dot_general(kmean_ref[:, cols], q, nt, preferred_element_type=F32,
                                                  precision=lax.Precision.HIGHEST)
        g_all = gate_ref[t, 0:n_blk, :]
        row = lax.broadcasted_iota(jnp.int32, g_all.shape, 0)
        rank = jnp.zeros(g_all.shape, jnp.int32)
        for j in range(n_blk):
            gj = gate_ref[t, j:j + 1, :]
            ahead = (gj > g_all) | ((gj == g_all) & (j < row))
            rank = rank + jnp.where(ahead & (j < qi), 1, 0)
        chosen = (row < qi) & (rank < MOBA_TOPK)
        gate_ref[t] = jnp.zeros(gate_ref.shape[1:], F32)
        gate_ref[t, 0:n_blk, :] = jnp.where(chosen, 0.0, NEG_INF)
        q_augs.append(jnp.concatenate([qb, gate_ref[t].T.astype(BF16)], axis=1))

        s = lax.dot_general(qb, k_ref[own, cols], nt, preferred_element_type=F32) * c2 + biasd_ref[t]
        m0 = jnp.max(s, axis=1, keepdims=True)
        p = jnp.exp2(s - m0)
        l0 = jnp.sum(p, axis=1, keepdims=True)
        acc0 = jnp.dot(p.astype(BF16), v_ref[own, cols], preferred_element_type=F32)
        carry0 += [m0, l0, acc0]

    def body(j, carry):
        page_unit(unit0 + 1 + j)
        rows = pl.ds(pl.multiple_of(j * blk, blk), blk)
        dist = (blk * (qi - j)).astype(F32)
        e_j = e_ref[rows, :]
        out = []
        for t in range(hp):
            m, l, acc = carry[3 * t:3 * t + 3]
            cols = slice(t * hd, (t + 1) * hd)
            k_aug = jnp.concatenate([k_ref[rows, cols], e_j], axis=1)
            s = lax.dot_general(q_augs[t], k_aug, nt, preferred_element_type=F32) * c2 + bias_ref[t]
            off = -slopes[t] * dist
            m_new = jnp.maximum(m, jnp.max(s, axis=1, keepdims=True) + off)
            alpha = jnp.exp2(m - m_new)
            p = jnp.exp2(s - (m_new - off))
            l = alpha * l + jnp.sum(p, axis=1, keepdims=True)
            acc = alpha * acc + jnp.dot(p.astype(BF16), v_ref[rows, cols], preferred_element_type=F32)
            out += [m_new, l, acc]
        return tuple(out)

    final = lax.fori_loop(0, qi, body, tuple(carry0))
    for t in range(hp):
        _, l, acc = final[3 * t:3 * t + 3]
        o_ref[:, t * hd:(t + 1) * hd] = (acc / l).astype(o_ref.dtype)


def moba_prompt(q, k_bf, v_bf, kmeans, b_sz, s_len, out_dtype, cache_k, layer, page_table_flat):
    t, width = q.shape
    hd = HEAD_DIM
    n_heads = width // hd
    hp = MOBA_HEADS_PER_STEP
    blk = MOBA_BLOCK
    n_blk = s_len // blk
    n_grp = n_heads // hp
    assert s_len % blk == 0 and n_blk % 8 == 0 and n_blk <= LANES and n_heads % hp == 0
    n_pages = page_table_flat.shape[0]
    pu = PAGE_UNIT
    units = b_sz * n_grp * (n_blk * (n_blk + 1) // 2)
    assert n_pages % pu == 0 and pu % PAGES_PER_BLOCK == 0 and units >= n_pages // pu + 2
    page_shape = cache_k.shape[2:]
    whole_seq = pl.BlockSpec((s_len, hp * hd), lambda b, g, i, pt: (b, g), pipeline_mode=pl.Buffered(1))
    return pl.pallas_call(
        functools.partial(_moba_prompt_kernel, n_blk=n_blk, n_heads=n_heads, hp=hp, layer=layer),
        out_shape=(jax.ShapeDtypeStruct((t, width), out_dtype),
                   jax.ShapeDtypeStruct((n_pages // PAGES_PER_BLOCK,) + page_shape[1:], F32)),
        grid_spec=pltpu.PrefetchScalarGridSpec(
            num_scalar_prefetch=1,
            grid=(b_sz, n_grp, n_blk),
            in_specs=[
                pl.BlockSpec((blk, hp * hd), lambda b, g, i, pt: (b * n_blk + i, g)),
                whole_seq,
                whole_seq,
                pl.BlockSpec((n_blk, hp * hd), lambda b, g, i, pt: (b, g)),
                pl.BlockSpec(memory_space=pl.ANY),
            ],
            out_specs=(pl.BlockSpec((blk, hp * hd), lambda b, g, i, pt: (b * n_blk + i, g)),
                       pl.BlockSpec(memory_space=pl.ANY)),
            scratch_shapes=[
                pltpu.VMEM((s_len, hd), BF16),
                pltpu.VMEM((hp, blk, blk), F32),
                pltpu.VMEM((hp, blk, blk), F32),
                pltpu.VMEM((hp, LANES, blk), F32),
                pltpu.VMEM((2, pu) + page_shape, F32),
                pltpu.VMEM((2, pu // PAGES_PER_BLOCK) + page_shape[1:], F32),
                pltpu.SemaphoreType.DMA((2,)),
                pltpu.SemaphoreType.DMA((2,)),
            ],
        ),
        compiler_params=_params("arbitrary", "arbitrary", "arbitrary"),
        name="moba_prompt",
    )(page_table_flat, q, k_bf, v_bf, kmeans, cache_k)


def _block_choice_kernel(bs_ref, q_ref, o_ref):
    bs = bs_ref[...]
    gate = jnp.broadcast_to(jnp.sum(bs * q_ref[...], axis=2, keepdims=True), bs.shape)
    blk_id = lax.broadcasted_iota(jnp.int32, bs.shape, 0).astype(F32)
    o_ref[...] = jnp.zeros_like(o_ref)
    for s in range(MOBA_TOPK):
        mx = jnp.max(gate, axis=0, keepdims=True)
        idx = jnp.min(jnp.where(gate == mx, blk_id, 1e9), axis=0, keepdims=True)
        o_ref[s] = idx[0].astype(jnp.int32)
        gate = jnp.where(blk_id == idx, -jnp.inf, gate)


def block_choice(blk_sums, q_s):
    n_seq, n_blk, n_heads, hd = blk_sums.shape
    return pl.pallas_call(
        _block_choice_kernel,
        out_shape=jax.ShapeDtypeStruct((n_seq, 8, n_heads, hd), jnp.int32),
        grid=(n_seq,),
        in_specs=[
            pl.BlockSpec((None, n_blk, n_heads, hd), lambda b: (b, 0, 0, 0)),
            pl.BlockSpec((None, n_heads, hd), lambda b: (b, 0, 0)),
        ],
        out_specs=pl.BlockSpec((None, 8, n_heads, hd), lambda b: (b, 0, 0, 0)),
        compiler_params=_params("parallel"),
        name="block_choice",
    )(blk_sums, q_s.reshape(n_seq, n_heads, hd))


def _sample_attend_kernel(phys_ref, idx_ref, q_ref, kn_ref, vn_ref, ck_ref, cv_ref, o_ref,
                          kbuf, vbuf, sem, *, n_heads, past_len, layer):
    n_sel = MOBA_TOPK * PAGES_PER_BLOCK
    n_keys = n_sel * PAGE_SIZE
    b = pl.program_id(0)
    nb = pl.num_programs(0)
    hd = HEAD_DIM
    scale = hd ** -0.5
    slot = lax.rem(b, 2)

    def copies(seq, slot_):
        out = []
        for h in range(n_heads):
            for s in range(n_sel):
                page = phys_ref[(seq * n_heads + h) * n_sel + s]
                rows = pl.ds(s * PAGE_SIZE, PAGE_SIZE)
                out.append(pltpu.make_async_copy(ck_ref.at[layer, page, :, h, :],
                                                 kbuf.at[slot_, h, rows, :], sem.at[0, slot_]))
                out.append(pltpu.make_async_copy(cv_ref.at[layer, page, :, h, :],
                                                 vbuf.at[slot_, h, rows, :], sem.at[1, slot_]))
        return out

    @pl.when(b == 0)
    def _():
        for cp in copies(b, slot):
            cp.start()

    @pl.when(b + 1 < nb)
    def _():
        for cp in copies(b + 1, 1 - slot):
            cp.start()

    for cp in copies(b, slot):
        cp.wait()

    key_row = lax.broadcasted_iota(jnp.int32, (PAGE_SIZE, hd), 0)
    for h in range(n_heads):
        slope = 2.0 ** (-8.0 * (h + 1) / n_heads)
        q = q_ref[h:h + 1, :]
        qk = lax.dot_general(kbuf[slot, h], jnp.broadcast_to(q, (hd, hd)), (((1,), (1,)), ((), ())),
                             preferred_element_type=F32, precision=lax.Precision.HIGHEST)
        dists = []
        for s in range(n_sel):
            blk = idx_ref[(b * n_heads + h) * MOBA_TOPK + s // PAGES_PER_BLOCK]
            pos0 = blk * MOBA_BLOCK + (s % PAGES_PER_BLOCK) * PAGE_SIZE
            dists.append((past_len - pos0 - key_row).astype(F32))
        sc = qk * scale - slope * jnp.concatenate(dists, axis=0)
        s_self = jnp.sum(q * kn_ref[h:h + 1, :], axis=1, keepdims=True) * scale
        m = jnp.maximum(jnp.max(sc, axis=0, keepdims=True), s_self)
        p = jnp.exp(sc - m)
        p_self = jnp.exp(s_self - m)
        l = jnp.sum(p, axis=0, keepdims=True) + p_self
        acc = jnp.sum(p * vbuf[slot, h], axis=0, keepdims=True) + p_self * vn_ref[h:h + 1, :]
        o_ref[h:h + 1, :] = acc / l


def sample_attend(q_s, k_s, v_s, cache_k, cache_v, layer, phys, idx_flat, past_len):
    n_seq, width = q_s.shape
    hd = HEAD_DIM
    n_heads = width // hd
    n_keys = MOBA_TOPK * MOBA_BLOCK
    row_spec = pl.BlockSpec((None, n_heads, hd), lambda b, ph, ix: (b, 0, 0))
    to3 = lambda a: a.reshape(n_seq, n_heads, hd)
    out = pl.pallas_call(
        functools.partial(_sample_attend_kernel, n_heads=n_heads, past_len=past_len, layer=layer),
        out_shape=jax.ShapeDtypeStruct((n_seq, n_heads, hd), F32),
        grid_spec=pltpu.PrefetchScalarGridSpec(
            num_scalar_prefetch=2,
            grid=(n_seq,),
            in_specs=[row_spec, row_spec, row_spec,
                      pl.BlockSpec(memory_space=pl.ANY), pl.BlockSpec(memory_space=pl.ANY)],
            out_specs=row_spec,
            scratch_shapes=[
                pltpu.VMEM((2, n_heads, n_keys, hd), F32),
                pltpu.VMEM((2, n_heads, n_keys, hd), F32),
                pltpu.SemaphoreType.DMA((2, 2)),
            ],
        ),
        compiler_params=_params("arbitrary"),
        name="sample_attend",
    )(phys, idx_flat, to3(q_s), to3(k_s), to3(v_s), cache_k, cache_v)
    return out.reshape(n_seq, width)


def _sgu_merge_kernel(x_ref, att_ref, u_ref, vg_ref, ga_ref, gb_ref, gs_ref, ws_ref, bs_ref,
                      pa_ref, pb_ref, wo_ref, *refs, chunked):
    if chunked:
        o_ref, m_ref, acc_ref = refs
    else:
        o_ref, vn_out_ref, m_ref, acc_ref = refs
    j = pl.program_id(1)
    tm = x_ref.shape[0]

    @pl.when(j == 0)
    def _():
        vn = _rms_rows(vg_ref[...].astype(F32), gs_ref[...])
        if chunked:
            ch = SGU_CHUNK
            gd = SGU_GROUP_DIM
            r = lax.broadcasted_iota(jnp.int32, (ch, ch), 0)
            c = lax.broadcasted_iota(jnp.int32, (ch, ch), 1)
            vnb = vn.astype(BF16)
            for g in range(ws_ref.shape[0]):
                w = jnp.where(c <= r, ws_ref[g], 0.0).astype(BF16)
                bias = bs_ref[g]
                for cc in range(tm // ch):
                    rows = slice(cc * ch, (cc + 1) * ch)
                    cols = slice(g * gd, (g + 1) * gd)
                    z = jnp.dot(w, vnb[rows, cols], preferred_element_type=F32) + bias
                    m_ref[rows, cols] = (u_ref[rows, cols].astype(F32) * z).astype(m_ref.dtype)
        else:
            vn_out_ref[...] = vn
            z = vn * ws_ref[...] + bs_ref[...]
            m_ref[...] = (u_ref[...].astype(F32) * z).astype(m_ref.dtype)

    pa = jnp.dot(att_ref[...].astype(BF16), pa_ref[...], preferred_element_type=F32)
    pb = jnp.dot(m_ref[...], pb_ref[...], preferred_element_type=F32)
    mixed = _sigmoid(ga_ref[...].astype(F32)) * pa + _sigmoid(gb_ref[...].astype(F32)) * pb
    part = jnp.dot(mixed.astype(BF16), wo_ref[...], preferred_element_type=F32)

    @pl.when(j == 0)
    def _():
        acc_ref[...] = part

    @pl.when(j != 0)
    def _():
        acc_ref[...] += part

    @pl.when(j == pl.num_programs(1) - 1)
    def _():
        o_ref[...] = x_ref[...] + acc_ref[...]


def sgu_merge(x, att, rest, g_sgu, w_sgu, b_sgu, w_pa, w_pb, w_o, chunked, tm, tn):
    t, d = x.shape
    a_w = att.shape[1]
    b_w = g_sgu.shape[0]
    tm = min(tm, t)
    assert d % tn == 0 and b_w % tn == 0 and a_w == b_w and tn == b_w
    nj = d // tn
    n_groups = w_sgu.shape[0]
    if chunked:
        assert tm % SGU_CHUNK == 0
        ws = w_sgu
        bs = jnp.broadcast_to(b_sgu[:, :, None], (n_groups, SGU_CHUNK, SGU_GROUP_DIM))
        ws_spec = pl.BlockSpec((n_groups, SGU_CHUNK, SGU_CHUNK), lambda i, j: (0, 0, 0))
        bs_spec = pl.BlockSpec((n_groups, SGU_CHUNK, SGU_GROUP_DIM), lambda i, j: (0, 0, 0))
    else:
        ws = jnp.repeat(w_sgu[:, 0, 0], SGU_GROUP_DIM).reshape(1, b_w)
        bs = jnp.repeat(b_sgu[:, 0], SGU_GROUP_DIM).reshape(1, b_w)
        ws_spec = bs_spec = pl.BlockSpec((1, b_w), lambda i, j: (0, 0))
    u_blk = b_w // tn
    in_specs = [
        pl.BlockSpec((tm, d), lambda i, j: (i, 0)),
        pl.BlockSpec((tm, a_w), lambda i, j: (i, 0)),
        pl.BlockSpec((tm, b_w), lambda i, j: (i, 0)),
        pl.BlockSpec((tm, b_w), lambda i, j: (i, 1)),
        pl.BlockSpec((tm, tn), lambda i, j: (i, 2 * u_blk + j)),
        pl.BlockSpec((tm, tn), lambda i, j: (i, 2 * u_blk + nj + j)),
        pl.BlockSpec((1, b_w), lambda i, j: (0, 0)),
        ws_spec, bs_spec,
        pl.BlockSpec((a_w, tn), lambda i, j: (0, j)),
        pl.BlockSpec((b_w, tn), lambda i, j: (0, j)),
        pl.BlockSpec((tn, d), lambda i, j: (j, 0)),
    ]
    x_spec = pl.BlockSpec((tm, d), lambda i, j: (i, 0))
    if chunked:
        out_shape = jax.ShapeDtypeStruct((t, d), F32)
        out_specs = x_spec
    else:
        out_shape = (jax.ShapeDtypeStruct((t, d), F32), jax.ShapeDtypeStruct((t, b_w), F32))
        out_specs = (x_spec, pl.BlockSpec((tm, b_w), lambda i, j: (i, 0)))
    return pl.pallas_call(
        functools.partial(_sgu_merge_kernel, chunked=chunked),
        out_shape=out_shape,
        grid=(t // tm, nj),
        in_specs=in_specs,
        out_specs=out_specs,
        scratch_shapes=[pltpu.VMEM((tm, b_w), BF16), pltpu.VMEM((tm, d), F32)],
        compiler_params=_params("parallel", "arbitrary"),
        name="sgu_merge",
    )(x, att, rest, rest, rest, rest, g_sgu.reshape(1, b_w), ws, bs, w_pa, w_pb, w_o)


def _moe_route_kernel(x_ref, g_ref, wr_ref, br_ref, xn_ref, comb_ref, *, as_list):
    xn = _rms_rows(x_ref[...], g_ref[...])
    xn_ref[...] = xn.astype(xn_ref.dtype)
    logits = jnp.dot(xn, wr_ref[...], preferred_element_type=F32,
                     precision=lax.Precision.HIGHEST) + br_ref[...]
    lane = lax.broadcasted_iota(jnp.int32, logits.shape, 1)
    lanef = lane.astype(F32)
    e0 = ROUTER_EXPERT_LANE0
    is_grp = lane < e0
    gl = jnp.where(is_grp, logits, -jnp.inf)
    gmax = jnp.max(gl, axis=1, keepdims=True)
    grp = jnp.min(jnp.where(gl == gmax, lanef, 1e9), axis=1, keepdims=True)
    p_grp = 1.0 / jnp.sum(jnp.where(is_grp, jnp.exp(logits - gmax), 0.0), axis=1, keepdims=True)
    lane_grp = jnp.right_shift(lane - e0, 2).astype(F32)
    in_grp = (lane >= e0) & (lane < e0 + N_EXPERTS) & (lane_grp == grp)
    el = jnp.where(in_grp, logits, -jnp.inf)
    t1 = jnp.max(el, axis=1, keepdims=True)
    i1 = jnp.min(jnp.where(el == t1, lanef, 1e9), axis=1, keepdims=True)
    el2 = jnp.where(lanef == i1, -jnp.inf, el)
    t2 = jnp.max(el2, axis=1, keepdims=True)
    i2 = jnp.min(jnp.where(el2 == t2, lanef, 1e9), axis=1, keepdims=True)
    e2 = jnp.exp(t2 - t1)
    w1 = (1.0 / (1.0 + e2)) * p_grp
    w2 = (e2 / (1.0 + e2)) * p_grp
    if as_list:
        comb_ref[...] = jnp.where(lane == 0, i1 - e0, jnp.where(lane == 1, i2 - e0,
                                  jnp.where(lane == 2, w1, jnp.where(lane == 3, w2, 0.0))))
    else:
        comb_ref[...] = jnp.where(lanef == i1, w1, jnp.where(lanef == i2, w2, 0.0))


def moe_route(x, g_ffn, w_rg, b_rg, w_re, b_re, tm, as_list):
    t, d = x.shape
    tm = min(tm, t)
    assert EXP_PER_GROUP == 4 and w_rg.shape[1] == N_EXP_GROUPS and w_re.shape[1] == N_EXPERTS
    pad = LANES - N_EXP_GROUPS - N_EXPERTS
    wr = jnp.concatenate([w_rg, w_re, jnp.zeros((d, pad), F32)], axis=1)
    br = jnp.concatenate([b_rg, b_re, jnp.zeros((pad,), F32)]).reshape(1, LANES)
    return pl.pallas_call(
        functools.partial(_moe_route_kernel, as_list=as_list),
        out_shape=(jax.ShapeDtypeStruct((t, d), F32 if as_list else BF16),
                   jax.ShapeDtypeStruct((t, LANES), F32)),
        grid=(t // tm,),
        in_specs=[
            pl.BlockSpec((tm, d), lambda i: (i, 0)),
            pl.BlockSpec((1, d), lambda i: (0, 0)),
            pl.BlockSpec((d, LANES), lambda i: (0, 0)),
            pl.BlockSpec((1, LANES), lambda i: (0, 0)),
        ],
        out_specs=(pl.BlockSpec((tm, d), lambda i: (i, 0)), pl.BlockSpec((tm, LANES), lambda i: (i, 0))),
        compiler_params=_params("parallel"),
        name="moe_route",
    )(x, g_ffn.reshape(1, d), wr, br)


def _moe_experts_kernel(xn_ref, comb_ref, x_ref, wg_ref, wu_ref, wd_ref, gf_ref, o_ref, acc_ref):
    e = pl.program_id(1)
    lane = lax.broadcasted_iota(jnp.int32, comb_ref.shape, 1)
    c = jnp.sum(jnp.where(lane == e + ROUTER_EXPERT_LANE0, comb_ref[...], 0.0), axis=1, keepdims=True)
    xn = xn_ref[...]
    g = jnp.dot(xn, wg_ref[...].astype(BF16), preferred_element_type=F32)
    u = jnp.dot(xn, wu_ref[...].astype(BF16), preferred_element_type=F32)
    h = (g * _sigmoid(g)) * u
    part = jnp.dot((h * c).astype(BF16), wd_ref[...].astype(BF16), preferred_element_type=F32)

    @pl.when(e == 0)
    def _():
        acc_ref[...] = part

    @pl.when(e != 0)
    def _():
        acc_ref[...] += part

    @pl.when(e == pl.num_programs(1) - 1)
    def _():
        o_ref[...] = _rms_rows(x_ref[...] + acc_ref[...], gf_ref[...])


def moe_experts(xn, comb, x, w_gate, w_up, w_down, g_final, tm):
    t, d = x.shape
    n_exp, _, f = w_gate.shape
    tm = min(tm, t)
    return pl.pallas_call(
        _moe_experts_kernel,
        out_shape=jax.ShapeDtypeStruct((t, d), F32),
        grid=(t // tm, n_exp),
        in_specs=[
            pl.BlockSpec((tm, d), lambda i, e: (i, 0)),
            pl.BlockSpec((tm, LANES), lambda i, e: (i, 0)),
            pl.BlockSpec((tm, d), lambda i, e: (i, 0)),
            pl.BlockSpec((None, d, f), lambda i, e: (e, 0, 0)),
            pl.BlockSpec((None, d, f), lambda i, e: (e, 0, 0)),
            pl.BlockSpec((None, f, d), lambda i, e: (e, 0, 0)),
            pl.BlockSpec((1, d), lambda i, e: (0, 0)),
        ],
        out_specs=pl.BlockSpec((tm, d), lambda i, e: (i, 0)),
        scratch_shapes=[pltpu.VMEM((tm, d), F32)],
        compiler_params=_params("parallel", "arbitrary"),
        name="moe_experts",
    )(xn, comb, x, w_gate, w_up, w_down, g_final.reshape(1, d))


def _dispatch_plan(e_tok, tm):
    n_pairs = e_tok.size
    n_exp = N_EXPERTS
    i32 = jnp.int32
    ef = e_tok.reshape(-1)
    order = jnp.argsort(ef, stable=True).astype(i32)
    inv = jnp.argsort(order).astype(i32)
    sorted_e = ef[order]
    experts = jnp.arange(n_exp, dtype=i32)
    c_start = jnp.sum((sorted_e[None, :] < experts[:, None]).astype(i32), axis=1)
    counts = jnp.concatenate([c_start[1:], jnp.full((1,), n_pairs, i32)]) - c_start
    padded = ((counts + tm - 1) // tm) * tm
    g_end = jnp.cumsum(padded).astype(i32)
    g_start = g_end - padded
    shift = g_start - c_start
    pair_row = inv + shift[ef]
    n_rows = n_pairs + n_exp * tm
    n_tiles = n_rows // tm
    tile_start = jnp.arange(n_tiles, dtype=i32) * tm
    tile_expert = jnp.minimum(jnp.sum((g_end[None, :] <= tile_start[:, None]).astype(i32), axis=1),
                              n_exp - 1)
    row = jnp.arange(n_rows, dtype=i32)
    s_pos = row - jnp.repeat(shift[tile_expert], tm)
    row_token = order[jnp.clip(s_pos, 0, n_pairs - 1)] // 2
    return row_token.astype(i32), tile_expert.astype(i32), pair_row.astype(i32)


def _row_gather(src_hbm, dst, sem, row_ids, base, n_rows, slot):
    def body(r, carry):
        tok = row_ids[base + r]
        pltpu.make_async_copy(src_hbm.at[pl.ds(tok, 1), :], dst.at[slot, pl.ds(r, 1), :], sem.at[slot]).start()
        return carry
    lax.fori_loop(0, n_rows, body, 0, unroll=8)


def _row_gather_static(src_hbm, dst, sem, row_ids, base, n_rows, slot):
    for r in range(n_rows):
        tok = row_ids[base + r]
        pltpu.make_async_copy(src_hbm.at[pl.ds(tok, 1), :], dst.at[slot, pl.ds(r, 1), :], sem.at[slot]).start()


def _row_gather_wait(src_hbm, dst, sem, n_rows, slot):
    def body(r, carry):
        pltpu.make_async_copy(src_hbm.at[pl.ds(0, 1), :], dst.at[slot, pl.ds(r, 1), :], sem.at[slot]).wait()
        return carry
    lax.fori_loop(0, n_rows, body, 0, unroll=8)


def _moe_sparse_kernel(tok_ref, texp_ref, xn_hbm, wg_ref, wu_ref, wd_ref, o_ref, xbuf, wgb, wub, wdb, sem):
    i = pl.program_id(0)
    n = pl.num_programs(0)
    tm = o_ref.shape[0]
    ahead = GATHER_SLOTS - 1

    @pl.when(i == 0)
    def _():
        for t in range(ahead):
            _row_gather(xn_hbm, xbuf, sem, tok_ref, jnp.minimum(t, n - 1) * tm, tm, t)

    @pl.when((i == 0) | (texp_ref[i] != texp_ref[jnp.maximum(i - 1, 0)]))
    def _():
        wgb[...] = wg_ref[...].astype(BF16)
        wub[...] = wu_ref[...].astype(BF16)
        wdb[...] = wd_ref[...].astype(BF16)

    slot = lax.rem(i, GATHER_SLOTS)
    _row_gather_wait(xn_hbm, xbuf, sem, tm, slot)
    xs = xbuf[slot].astype(BF16)
    g = jnp.dot(xs, wgb[...], preferred_element_type=F32)
    u = jnp.dot(xs, wub[...], preferred_element_type=F32)
    hb = ((g * _sigmoid(g)) * u).astype(BF16)
    o_ref[...] = jnp.dot(hb, wdb[...], preferred_element_type=F32)

    _row_gather_static(xn_hbm, xbuf, sem, tok_ref, jnp.minimum(i + ahead, n - 1) * tm, tm,
                       lax.rem(i + ahead, GATHER_SLOTS))

    @pl.when(i == n - 1)
    def _():
        for t in range(1, GATHER_SLOTS):
            _row_gather_wait(xn_hbm, xbuf, sem, tm, lax.rem(i + t, GATHER_SLOTS))


def moe_sparse_experts(xn, row_token, tile_expert, w_gate, w_up, w_down, tm):
    t, d = xn.shape
    n_exp, _, f = w_gate.shape
    n_rows = row_token.shape[0]
    return pl.pallas_call(
        _moe_sparse_kernel,
        out_shape=jax.ShapeDtypeStruct((n_rows, d), F32),
        grid_spec=pltpu.PrefetchScalarGridSpec(
            num_scalar_prefetch=2,
            grid=(n_rows // tm,),
            in_specs=[
                pl.BlockSpec(memory_space=pl.ANY),
                pl.BlockSpec((None, d, f), lambda i, tok, te: (te[i], 0, 0)),
                pl.BlockSpec((None, d, f), lambda i, tok, te: (te[i], 0, 0)),
                pl.BlockSpec((None, f, d), lambda i, tok, te: (te[i], 0, 0)),
            ],
            out_specs=pl.BlockSpec((tm, d), lambda i, tok, te: (i, 0)),
            scratch_shapes=[pltpu.VMEM((GATHER_SLOTS, tm, d), F32), pltpu.VMEM((d, f), BF16),
                            pltpu.VMEM((d, f), BF16), pltpu.VMEM((f, d), BF16),
                            pltpu.SemaphoreType.DMA((GATHER_SLOTS,))],
        ),
        compiler_params=_params("arbitrary"),
        name="moe_sparse_experts",
    )(row_token, tile_expert, xn, w_gate, w_up, w_down)


def _moe_combine_kernel(row_ref, ys_hbm, x_ref, route_ref, gf_ref, o_ref, ybuf, sem):
    i = pl.program_id(0)
    n = pl.num_programs(0)
    tm = o_ref.shape[0]
    ahead = GATHER_SLOTS - 1

    @pl.when(i == 0)
    def _():
        for t in range(ahead):
            _row_gather(ys_hbm, ybuf, sem, row_ref, jnp.minimum(t, n - 1) * 2 * tm, 2 * tm, t)

    slot = lax.rem(i, GATHER_SLOTS)
    _row_gather_wait(ys_hbm, ybuf, sem, 2 * tm, slot)
    k = TOP_K_IN_GROUP
    route = route_ref[...]
    lane = lax.broadcasted_iota(jnp.int32, route.shape, 1)
    w1 = jnp.sum(jnp.where(lane == k, route, 0.0), axis=1, keepdims=True)
    w2 = jnp.sum(jnp.where(lane == k + 1, route, 0.0), axis=1, keepdims=True)
    y = w1 * ybuf[slot, 0:tm, :] + w2 * ybuf[slot, tm:2 * tm, :]
    o_ref[...] = _rms_rows(x_ref[...] + y, gf_ref[...])

    _row_gather_static(ys_hbm, ybuf, sem, row_ref, jnp.minimum(i + ahead, n - 1) * 2 * tm, 2 * tm,
                       lax.rem(i + ahead, GATHER_SLOTS))

    @pl.when(i == n - 1)
    def _():
        for t in range(1, GATHER_SLOTS):
            _row_gather_wait(ys_hbm, ybuf, sem, 2 * tm, lax.rem(i + t, GATHER_SLOTS))


def moe_combine(ys, pair_row, x, route, g_final, tm):
    t, d = x.shape
    rows = jnp.transpose(pair_row.reshape(t // tm, tm, 2), (0, 2, 1)).reshape(-1)
    return pl.pallas_call(
        _moe_combine_kernel,
        out_shape=jax.ShapeDtypeStruct((t, d), F32),
        grid_spec=pltpu.PrefetchScalarGridSpec(
            num_scalar_prefetch=1,
            grid=(t // tm,),
            in_specs=[
                pl.BlockSpec(memory_space=pl.ANY),
                pl.BlockSpec((tm, d), lambda i, rw: (i, 0)),
                pl.BlockSpec((tm, LANES), lambda i, rw: (i, 0)),
                pl.BlockSpec((1, d), lambda i, rw: (0, 0)),
            ],
            out_specs=pl.BlockSpec((tm, d), lambda i, rw: (i, 0)),
            scratch_shapes=[pltpu.VMEM((GATHER_SLOTS, 2 * tm, d), F32),
                            pltpu.SemaphoreType.DMA((GATHER_SLOTS,))],
        ),
        compiler_params=_params("arbitrary"),
        name="moe_combine",
    )(rows, ys, x, route, g_final.reshape(1, d))


def kernel(x_prompt, x_sample, cache_k, cache_v, page_table, g_mix, w_in, g_sgu, w_sgu, b_sgu, w_proj_a, w_proj_b, w_out, g_ffn, w_router_group, b_router_group, w_router_expert, b_router_expert, w_exp_gate, w_exp_up, w_exp_down, g_final):
    depth = g_mix.shape[0]
    assert depth == 1
    layer = 0
    b_sz, s_len, d = x_prompt.shape
    n_seq, n_new, _ = x_sample.shape
    assert n_new == 1
    _, n_pool, page, n_heads, hd = cache_k.shape
    assert page == PAGE_SIZE and hd == HEAD_DIM
    a_w = n_heads * hd
    b_w = g_sgu.shape[1]
    n_pages = page_table.shape[1]
    past_len = n_pages * PAGE_SIZE
    assert past_len % MOBA_BLOCK == 0 and n_pages // PAGES_PER_BLOCK >= MOBA_TOPK

    w_in_l = w_in[layer]
    w_qkv = w_in_l[:, 0:3 * a_w].astype(BF16)
    w_q, w_k, w_v = (w_qkv[:, c * a_w:(c + 1) * a_w] for c in range(3))
    w_rest = w_in_l[:, 3 * a_w:].astype(BF16)
    w_pa = w_proj_a[layer].astype(BF16)
    w_pb = w_proj_b[layer].astype(BF16)
    w_o = w_out[layer].astype(BF16)
    w_eg, w_eu, w_ed = w_exp_gate[layer], w_exp_up[layer], w_exp_down[layer]

    def dense_tail(x2, att, rest, chunked):
        merged = sgu_merge(x2, att, rest, g_sgu[layer], w_sgu[layer], b_sgu[layer], w_pa, w_pb, w_o,
                           chunked=chunked, tm=512, tn=1024)
        x1, vn = (merged, None) if chunked else merged
        sparse = x1.shape[0] >= MOE_SPARSE_MIN_TOKENS
        xn, route = moe_route(x1, g_ffn[layer], w_router_group[layer], b_router_group[layer],
                              w_router_expert[layer], b_router_expert[layer], tm=256, as_list=sparse)
        if sparse:
            e_tok = route[:, 0:TOP_K_IN_GROUP].astype(jnp.int32)
            row_token, tile_expert, pair_row = _dispatch_plan(e_tok, MOE_ROW_TILE)
            ys = moe_sparse_experts(xn, row_token, tile_expert, w_eg, w_eu, w_ed, tm=MOE_ROW_TILE)
            y = moe_combine(ys, pair_row.reshape(-1, TOP_K_IN_GROUP), x1, route, g_final, tm=MOE_ROW_TILE)
        else:
            y = moe_experts(xn, route, x1, w_eg, w_eu, w_ed, g_final, tm=512)
        return y, vn

    xp = x_prompt.reshape(b_sz * s_len, d)
    g_in = g_mix[layer]
    q_p, k_p, v_p, kb_p, vb_p = norm_qkv(xp, g_in, w_qkv, tm=512)
    (rest_p,) = norm_matmul(xp, g_in, w_rest, (BF16,), tm=1024, tn=1024)
    pt_flat = page_table.reshape(-1)
    att_p, blk_sums = moba_prompt(q_p, kb_p, vb_p, block_means(k_p), b_sz, s_len, BF16,
                                  cache_k, layer, pt_flat)
    y_p, _ = dense_tail(xp, att_p, rest_p, chunked=True)

    xs = x_sample.reshape(n_seq, d)
    (q_s,) = norm_matmul(xs, g_in, w_q, (F32,), tm=n_seq, tn=1024)
    (k_s,) = norm_matmul(xs, g_in, w_k, (F32,), tm=n_seq, tn=1024)
    (v_s,) = norm_matmul(xs, g_in, w_v, (F32,), tm=n_seq, tn=1024)
    (rest_s,) = norm_matmul(xs, g_in, w_rest, (F32,), tm=n_seq, tn=1024)
    blk_sums = blk_sums.reshape(n_seq, n_pages // PAGES_PER_BLOCK, n_heads, hd)
    picks = block_choice(blk_sums, q_s)
    idx = jnp.transpose(picks[:, :MOBA_TOPK, :, 0], (0, 2, 1))
    pages = idx[..., None] * PAGES_PER_BLOCK + jnp.arange(PAGES_PER_BLOCK, dtype=jnp.int32)
    phys = jnp.take_along_axis(page_table[:, None, :], pages.reshape(n_seq, n_heads, -1), axis=2)
    att_s = sample_attend(q_s, k_s, v_s, cache_k, cache_v, layer, phys.reshape(-1), idx.reshape(-1),
                          past_len)
    y_s, vn_s = dense_tail(xs, att_s, rest_s, chunked=False)

    return (
        y_p.reshape(b_sz, s_len, d),
        y_s.reshape(n_seq, n_new, d),
        k_p.reshape(depth, b_sz, s_len, n_heads, hd),
        v_p.reshape(depth, b_sz, s_len, n_heads, hd),
        k_s.reshape(depth, n_seq, n_new, n_heads, hd),
        v_s.reshape(depth, n_seq, n_new, n_heads, hd),
        vn_s.reshape(depth, n_seq, n_new, b_w),
    )
```

```python
import functools

import jax
import jax.numpy as jnp
from jax import lax
from jax.experimental import pallas as pl
from jax.experimental.pallas import tpu as pltpu

NORM_EPS = 1e-6
NEG_INF = -1e30
HEAD_DIM = 128
MOBA_BLOCK = 256
MOBA_TOPK = 3
PAGE_SIZE = 128
PAGES_PER_BLOCK = MOBA_BLOCK // PAGE_SIZE
SGU_CHUNK = 128
SGU_GROUP_DIM = 128
N_EXP_GROUPS = 4
EXP_PER_GROUP = 4
N_EXPERTS = N_EXP_GROUPS * EXP_PER_GROUP
TOP_K_IN_GROUP = 2
MOE_ROW_TILE = 256
MOE_SPARSE_MIN_TOKENS = 2048
GATHER_SLOTS = 3
LANES = 128
ROUTER_EXPERT_LANE0 = N_EXP_GROUPS
VMEM_LIMIT_BYTES = 56 * 1024 * 1024
PAGE_UNIT = 8
MOBA_HEADS_PER_STEP = 8
LOG2E = 1.4426950408889634

F32 = jnp.float32
BF16 = jnp.bfloat16


def _params(*semantics):
    return pltpu.CompilerParams(dimension_semantics=semantics, vmem_limit_bytes=VMEM_LIMIT_BYTES)


def _rms_rows(x, g):
    inv = lax.rsqrt(jnp.mean(x * x, axis=-1, keepdims=True) + NORM_EPS)
    return (x * inv) * g


def _sigmoid(x):
    return 1.0 / (1.0 + jnp.exp(-x))


def _norm_matmul_kernel(x_ref, g_ref, w_ref, *refs, row_chunk):
    out_refs, xn_ref = refs[:-1], refs[-1]

    @pl.when(pl.program_id(1) == 0)
    def _():
        def body(r, carry):
            rows = pl.ds(pl.multiple_of(r * row_chunk, row_chunk), row_chunk)
            xn_ref[rows, :] = _rms_rows(x_ref[rows, :], g_ref[...]).astype(xn_ref.dtype)
            return carry
        lax.fori_loop(0, x_ref.shape[0] // row_chunk, body, 0)

    res = jnp.dot(xn_ref[...], w_ref[...], preferred_element_type=F32)
    for o_ref in out_refs:
        o_ref[...] = res.astype(o_ref.dtype)


def _norm_qkv_kernel(x_ref, g_ref, w_ref, q_ref, k_ref, v_ref, kb_ref, vb_ref, xn_ref, *, row_chunk):
    j = pl.program_id(1)

    @pl.when(j == 0)
    def _():
        def body(r, carry):
            rows = pl.ds(pl.multiple_of(r * row_chunk, row_chunk), row_chunk)
            xn_ref[rows, :] = _rms_rows(x_ref[rows, :], g_ref[...]).astype(xn_ref.dtype)
            return carry
        lax.fori_loop(0, x_ref.shape[0] // row_chunk, body, 0)

    res = jnp.dot(xn_ref[...], w_ref[...], preferred_element_type=F32)

    @pl.when(j == 0)
    def _():
        q_ref[...] = res

    @pl.when(j == 1)
    def _():
        k_ref[...] = res
        kb_ref[...] = res.astype(kb_ref.dtype)

    @pl.when(j == 2)
    def _():
        v_ref[...] = res
        vb_ref[...] = res.astype(vb_ref.dtype)


def norm_qkv(x, g, w_qkv, tm):
    t, d = x.shape
    a_w = w_qkv.shape[1] // 3
    tm = min(tm, t)
    row_chunk = min(tm, 128)
    out_spec = pl.BlockSpec((tm, a_w), lambda i, j: (i, 0))
    return pl.pallas_call(
        functools.partial(_norm_qkv_kernel, row_chunk=row_chunk),
        out_shape=tuple(jax.ShapeDtypeStruct((t, a_w), dt) for dt in (F32, F32, F32, BF16, BF16)),
        grid=(t // tm, 3),
        in_specs=[
            pl.BlockSpec((tm, d), lambda i, j: (i, 0)),
            pl.BlockSpec((1, d), lambda i, j: (0, 0)),
            pl.BlockSpec((d, a_w), lambda i, j: (0, j)),
        ],
        out_specs=tuple(out_spec for _ in range(5)),
        scratch_shapes=[pltpu.VMEM((tm, d), BF16)],
        compiler_params=_params("parallel", "arbitrary"),
        name="norm_qkv",
    )(x, g.reshape(1, d), w_qkv)


def norm_matmul(x, g, w, out_dtypes, tm, tn):
    t, d = x.shape
    n = w.shape[1]
    tm = min(tm, t)
    tn = min(tn, n)
    row_chunk = min(tm, 128)
    out_spec = pl.BlockSpec((tm, tn), lambda i, j: (i, j))
    return pl.pallas_call(
        functools.partial(_norm_matmul_kernel, row_chunk=row_chunk),
        out_shape=tuple(jax.ShapeDtypeStruct((t, n), dt) for dt in out_dtypes),
        grid=(t // tm, n // tn),
        in_specs=[
            pl.BlockSpec((tm, d), lambda i, j: (i, 0)),
            pl.BlockSpec((1, d), lambda i, j: (0, 0)),
            pl.BlockSpec((d, tn), lambda i, j: (0, j)),
        ],
        out_specs=tuple(out_spec for _ in out_dtypes),
        scratch_shapes=[pltpu.VMEM((tm, d), BF16)],
        compiler_params=_params("parallel", "arbitrary"),
        name="norm_matmul",
    )(x, g.reshape(1, d), w)


def _block_means_kernel(k_ref, o_ref):
    blk = MOBA_BLOCK
    for n in range(o_ref.shape[0]):
        o_ref[n:n + 1, :] = jnp.sum(k_ref[n * blk:(n + 1) * blk, :], axis=0, keepdims=True) * (1.0 / blk)


def block_means(k):
    t, width = k.shape
    per_step = 8
    rows = per_step * MOBA_BLOCK
    assert t % rows == 0
    return pl.pallas_call(
        _block_means_kernel,
        out_shape=jax.ShapeDtypeStruct((t // MOBA_BLOCK, width), F32),
        grid=(t // rows,),
        in_specs=[pl.BlockSpec((rows, width), lambda i: (i, 0))],
        out_specs=pl.BlockSpec((per_step, width), lambda i: (i, 0)),
        compiler_params=_params("parallel"),
        name="block_means",
    )(k)


def _moba_prompt_kernel(pt_ref, q_ref, k_ref, v_ref, kmean_ref, ck_hbm, o_ref, sums_hbm,
                        e_ref, bias_ref, biasd_ref, gate_ref, pbuf, stage, psem, osem,
                        *, n_blk, n_heads, hp, layer):
    blk = MOBA_BLOCK
    hd = HEAD_DIM
    bat = pl.program_id(0)
    grp = pl.program_id(1)
    qi = pl.program_id(2)
    n_grp = pl.num_programs(1)
    c2 = hd ** -0.5 * LOG2E
    nt = (((1,), (1,)), ((), ()))
    slopes = [jnp.exp2(jnp.full((1, 1), -8.0 / n_heads, F32) * (grp * hp + t + 1).astype(F32)) * LOG2E
              for t in range(hp)]

    pu = PAGE_UNIT
    units_used = pt_ref.shape[0] // pu
    sums_per_unit = pu // PAGES_PER_BLOCK
    unit0 = (bat * n_grp + grp) * (n_blk * (n_blk + 1) // 2) + (qi * (qi + 1)) // 2

    def page_copies(u, slot):
        return [pltpu.make_async_copy(ck_hbm.at[layer, pt_ref[u * pu + r]], pbuf.at[slot, r], psem.at[slot])
                for r in range(pu)]

    def sums_copy(u, slot):
        return pltpu.make_async_copy(stage.at[slot], sums_hbm.at[pl.ds(u * sums_per_unit, sums_per_unit)],
                                     osem.at[slot])

    def page_unit(u):
        slot = lax.rem(u, 2)

        @pl.when((u >= 2) & (u - 2 < units_used))
        def _():
            sums_copy(u - 2, slot).wait()

        @pl.when(u < units_used)
        def _():
            @pl.when(u == 0)
            def _():
                for cp in page_copies(u, slot):
                    cp.start()

            @pl.when(u + 1 < units_used)
            def _():
                for cp in page_copies(u + 1, 1 - slot):
                    cp.start()

            for cp in page_copies(u, slot):
                cp.wait()
            for i in range(sums_per_unit):
                stage[slot, i] = sum(jnp.sum(pbuf[slot, PAGES_PER_BLOCK * i + r], axis=0)
                                     for r in range(PAGES_PER_BLOCK))
            sums_copy(u, slot).start()

    page_unit(unit0)

    @pl.when(qi == 0)
    def _():
        lane = lax.broadcasted_iota(jnp.int32, (blk, hd), 1)
        for n in range(n_blk):
            e_ref[n * blk:(n + 1) * blk, :] = jnp.where(lane == n, 1.0, 0.0).astype(BF16)
        r = lax.broadcasted_iota(jnp.int32, (blk, blk), 0)
        c = lax.broadcasted_iota(jnp.int32, (blk, blk), 1)
        for t in range(hp):
            alibi = -slopes[t] * (r - c).astype(F32)
            bias_ref[t] = alibi
            biasd_ref[t] = jnp.where(c <= r, alibi, NEG_INF)

    own = pl.ds(pl.multiple_of(qi * blk, blk), blk)
    ones = jnp.ones((blk, hd), BF16)
    q_augs, carry0 = [], []
    for t in range(hp):
        cols = slice(t * hd, (t + 1) * hd)
        q = q_ref[:, cols]
        qb = q.astype(BF16)
        gate_ref[t, 0:n_blk, :] = lax.dot_general(kmean_ref[:, cols], q, nt, preferred_element_type=F32,
                                                  precision=lax.Precision.HIGHEST)
        g_all = gate_ref[t, 0:n_blk, :]
        row = lax.broadcasted_iota(jnp.int32, g_all.shape, 0)
        rank = jnp.zeros(g_all.shape, jnp.int32)
        for j in range(n_blk):
            gj = gate_ref[t, j:j + 1, :]
            ahead = (gj > g_all) | ((gj == g_all) & (j < row))
            rank = rank + jnp.where(ahead & (j < qi), 1, 0)
        chosen = (row < qi) & (rank < MOBA_TOPK)
        gate_ref[t] = jnp.zeros(gate_ref.shape[1:], F32)
        gate_ref[t, 0:n_blk, :] = jnp.where(chosen, 0.0, NEG_INF)
        q_augs.append(jnp.concatenate([qb, gate_ref[t].T.astype(BF16)], axis=1))

        s = lax.dot_general(qb, k_ref[own, cols], nt, preferred_element_type=F32) * c2 + biasd_ref[t]
        m0 = jnp.max(s, axis=1, keepdims=True)
        p = jnp.exp2((s - m0).astype(BF16))
        v_aug = jnp.concatenate([v_ref[own, cols], ones], axis=1)
        carry0 += [m0, jnp.dot(p, v_aug, preferred_element_type=F32)]

    def body(j, carry):
        page_unit(unit0 + 1 + j)
        rows = pl.ds(pl.multiple_of(j * blk, blk), blk)
        dist = (blk * (qi - j)).astype(F32)
        e_j = e_ref[rows, :]
        out = []
        for t in range(hp):
            m, acc = carry[2 * t:2 * t + 2]
            cols = slice(t * hd, (t + 1) * hd)
            k_aug = jnp.concatenate([k_ref[rows, cols], e_j], axis=1)
            s = lax.dot_general(q_augs[t], k_aug, nt, preferred_element_type=F32) * c2 + bias_ref[t]
            off = -slopes[t] * dist
            m_new = jnp.maximum(m, jnp.max(s, axis=1, keepdims=True) + off)
            alpha = jnp.exp2(m - m_new)
            p = jnp.exp2((s - (m_new - off)).astype(BF16))
            v_aug = jnp.concatenate([v_ref[rows, cols], ones], axis=1)
            acc = alpha * acc + jnp.dot(p, v_aug, preferred_element_type=F32)
            out += [m_new, acc]
        return tuple(out)

    final = lax.fori_loop(0, qi, body, tuple(carry0))
    for t in range(hp):
        acc = final[2 * t + 1]
        o_ref[:, t * hd:(t + 1) * hd] = (acc[:, 0:hd] / acc[:, hd:2 * hd]).astype(o_ref.dtype)


def moba_prompt(q, k_bf, v_bf, kmeans, b_sz, s_len, out_dtype, cache_k, layer, page_table_flat):
    t, width = q.shape
    hd = HEAD_DIM
    n_heads = width // hd
    hp = MOBA_HEADS_PER_STEP
    blk = MOBA_BLOCK
    n_blk = s_len // blk
    n_grp = n_heads // hp
    assert s_len % blk == 0 and n_blk % 8 == 0 and n_blk <= LANES and n_heads % hp == 0
    n_pages = page_table_flat.shape[0]
    pu = PAGE_UNIT
    units = b_sz * n_grp * (n_blk * (n_blk + 1) // 2)
    assert n_pages % pu == 0 and pu % PAGES_PER_BLOCK == 0 and units >= n_pages // pu + 2
    page_shape = cache_k.shape[2:]
    whole_seq = pl.BlockSpec((s_len, hp * hd), lambda b, g, i, pt: (b, g), pipeline_mode=pl.Buffered(1))
    return pl.pallas_call(
        functools.partial(_moba_prompt_kernel, n_blk=n_blk, n_heads=n_heads, hp=hp, layer=layer),
        out_shape=(jax.ShapeDtypeStruct((t, width), out_dtype),
                   jax.ShapeDtypeStruct((n_pages // PAGES_PER_BLOCK,) + page_shape[1:], F32)),
        grid_spec=pltpu.PrefetchScalarGridSpec(
            num_scalar_prefetch=1,
            grid=(b_sz, n_grp, n_blk),
            in_specs=[
                pl.BlockSpec((blk, hp * hd), lambda b, g, i, pt: (b * n_blk + i, g)),
                whole_seq,
                whole_seq,
                pl.BlockSpec((n_blk, hp * hd), lambda b, g, i, pt: (b, g)),
                pl.BlockSpec(memory_space=pl.ANY),
            ],
            out_specs=(pl.BlockSpec((blk, hp * hd), lambda b, g, i, pt: (b * n_blk + i, g)),
                       pl.BlockSpec(memory_space=pl.ANY)),
            scratch_shapes=[
                pltpu.VMEM((s_len, hd), BF16),
                pltpu.VMEM((hp, blk, blk), F32),
                pltpu.VMEM((hp, blk, blk), F32),
                pltpu.VMEM((hp, LANES, blk), F32),
                pltpu.VMEM((2, pu) + page_shape, F32),
                pltpu.VMEM((2, pu // PAGES_PER_BLOCK) + page_shape[1:], F32),
                pltpu.SemaphoreType.DMA((2,)),
                pltpu.SemaphoreType.DMA((2,)),
            ],
        ),
        compiler_params=_params("arbitrary", "arbitrary", "arbitrary"),
        name="moba_prompt",
    )(page_table_flat, q, k_bf, v_bf, kmeans, cache_k)


def _block_choice_kernel(bs_ref, q_ref, o_ref):
    bs = bs_ref[...]
    gate = jnp.broadcast_to(jnp.sum(bs * q_ref[...], axis=2, keepdims=True), bs.shape)
    blk_id = lax.broadcasted_iota(jnp.int32, bs.shape, 0).astype(F32)
    o_ref[...] = jnp.zeros_like(o_ref)
    for s in range(MOBA_TOPK):
        mx = jnp.max(gate, axis=0, keepdims=True)
        idx = jnp.min(jnp.where(gate == mx, blk_id, 1e9), axis=0, keepdims=True)
        o_ref[s] = idx[0].astype(jnp.int32)
        gate = jnp.where(blk_id == idx, -jnp.inf, gate)


def block_choice(blk_sums, q_s):
    n_seq, n_blk, n_heads, hd = blk_sums.shape
    return pl.pallas_call(
        _block_choice_kernel,
        out_shape=jax.ShapeDtypeStruct((n_seq, 8, n_heads, hd), jnp.int32),
        grid=(n_seq,),
        in_specs=[
            pl.BlockSpec((None, n_blk, n_heads, hd), lambda b: (b, 0, 0, 0)),
            pl.BlockSpec((None, n_heads, hd), lambda b: (b, 0, 0)),
        ],
        out_specs=pl.BlockSpec((None, 8, n_heads, hd), lambda b: (b, 0, 0, 0)),
        compiler_params=_params("parallel"),
        name="block_choice",
    )(blk_sums, q_s.reshape(n_seq, n_heads, hd))


def _sample_attend_kernel(phys_ref, idx_ref, q_ref, kn_ref, vn_ref, ck_ref, cv_ref, o_ref,
                          kbuf, vbuf, sem, *, n_heads, past_len, layer):
    n_sel = MOBA_TOPK * PAGES_PER_BLOCK
    n_keys = n_sel * PAGE_SIZE
    b = pl.program_id(0)
    nb = pl.num_programs(0)
    hd = HEAD_DIM
    scale = hd ** -0.5
    slot = lax.rem(b, 2)

    def copies(seq, slot_):
        out = []
        for h in range(n_heads):
            for s in range(n_sel):
                page = phys_ref[(seq * n_heads + h) * n_sel + s]
                rows = pl.ds(s * PAGE_SIZE, PAGE_SIZE)
                out.append(pltpu.make_async_copy(ck_ref.at[layer, page, :, h, :],
                                                 kbuf.at[slot_, h, rows, :], sem.at[0, slot_]))
                out.append(pltpu.make_async_copy(cv_ref.at[layer, page, :, h, :],
                                                 vbuf.at[slot_, h, rows, :], sem.at[1, slot_]))
        return out

    @pl.when(b == 0)
    def _():
        for cp in copies(b, slot):
            cp.start()

    @pl.when(b + 1 < nb)
    def _():
        for cp in copies(b + 1, 1 - slot):
            cp.start()

    for cp in copies(b, slot):
        cp.wait()

    key_row = lax.broadcasted_iota(jnp.int32, (PAGE_SIZE, hd), 0)
    for h in range(n_heads):
        slope = 2.0 ** (-8.0 * (h + 1) / n_heads)
        q = q_ref[h:h + 1, :]
        qk = lax.dot_general(kbuf[slot, h], jnp.broadcast_to(q, (hd, hd)), (((1,), (1,)), ((), ())),
                             preferred_element_type=F32, precision=lax.Precision.HIGHEST)
        dists = []
        for s in range(n_sel):
            blk = idx_ref[(b * n_heads + h) * MOBA_TOPK + s // PAGES_PER_BLOCK]
            pos0 = blk * MOBA_BLOCK + (s % PAGES_PER_BLOCK) * PAGE_SIZE
            dists.append((past_len - pos0 - key_row).astype(F32))
        sc = qk * scale - slope * jnp.concatenate(dists, axis=0)
        s_self = jnp.sum(q * kn_ref[h:h + 1, :], axis=1, keepdims=True) * scale
        m = jnp.maximum(jnp.max(sc, axis=0, keepdims=True), s_self)
        p = jnp.exp(sc - m)
        p_self = jnp.exp(s_self - m)
        l = jnp.sum(p, axis=0, keepdims=True) + p_self
        acc = jnp.sum(p * vbuf[slot, h], axis=0, keepdims=True) + p_self * vn_ref[h:h + 1, :]
        o_ref[h:h + 1, :] = acc / l


def sample_attend(q_s, k_s, v_s, cache_k, cache_v, layer, phys, idx_flat, past_len):
    n_seq, width = q_s.shape
    hd = HEAD_DIM
    n_heads = width // hd
    n_keys = MOBA_TOPK * MOBA_BLOCK
    row_spec = pl.BlockSpec((None, n_heads, hd), lambda b, ph, ix: (b, 0, 0))
    to3 = lambda a: a.reshape(n_seq, n_heads, hd)
    out = pl.pallas_call(
        functools.partial(_sample_attend_kernel, n_heads=n_heads, past_len=past_len, layer=layer),
        out_shape=jax.ShapeDtypeStruct((n_seq, n_heads, hd), F32),
        grid_spec=pltpu.PrefetchScalarGridSpec(
            num_scalar_prefetch=2,
            grid=(n_seq,),
            in_specs=[row_spec, row_spec, row_spec,
                      pl.BlockSpec(memory_space=pl.ANY), pl.BlockSpec(memory_space=pl.ANY)],
            out_specs=row_spec,
            scratch_shapes=[
                pltpu.VMEM((2, n_heads, n_keys, hd), F32),
                pltpu.VMEM((2, n_heads, n_keys, hd), F32),
                pltpu.SemaphoreType.DMA((2, 2)),
            ],
        ),
        compiler_params=_params("arbitrary"),
        name="sample_attend",
    )(phys, idx_flat, to3(q_s), to3(k_s), to3(v_s), cache_k, cache_v)
    return out.reshape(n_seq, width)


def _sgu_merge_kernel(x_ref, att_ref, u_ref, vg_ref, ga_ref, gb_ref, gs_ref, ws_ref, bs_ref,
                      pa_ref, pb_ref, wo_ref, *refs, chunked):
    if chunked:
        o_ref, m_ref, acc_ref = refs
    else:
        o_ref, vn_out_ref, m_ref, acc_ref = refs
    j = pl.program_id(1)
    tm = x_ref.shape[0]

    @pl.when(j == 0)
    def _():
        vn = _rms_rows(vg_ref[...].astype(F32), gs_ref[...])
        if chunked:
            ch = SGU_CHUNK
            gd = SGU_GROUP_DIM
            r = lax.broadcasted_iota(jnp.int32, (ch, ch), 0)
            c = lax.broadcasted_iota(jnp.int32, (ch, ch), 1)
            vnb = vn.astype(BF16)
            for g in range(ws_ref.shape[0]):
                w = jnp.where(c <= r, ws_ref[g], 0.0).astype(BF16)
                bias = bs_ref[g]
                for cc in range(tm // ch):
                    rows = slice(cc * ch, (cc + 1) * ch)
                    cols = slice(g * gd, (g + 1) * gd)
                    z = jnp.dot(w, vnb[rows, cols], preferred_element_type=F32) + bias
                    m_ref[rows, cols] = (u_ref[rows, cols].astype(F32) * z).astype(m_ref.dtype)
        else:
            vn_out_ref[...] = vn
            z = vn * ws_ref[...] + bs_ref[...]
            m_ref[...] = (u_ref[...].astype(F32) * z).astype(m_ref.dtype)

    pa = jnp.dot(att_ref[...].astype(BF16), pa_ref[...], preferred_element_type=F32)
    pb = jnp.dot(m_ref[...], pb_ref[...], preferred_element_type=F32)
    mixed = _sigmoid(ga_ref[...].astype(F32)) * pa + _sigmoid(gb_ref[...].astype(F32)) * pb
    part = jnp.dot(mixed.astype(BF16), wo_ref[...], preferred_element_type=F32)

    @pl.when(j == 0)
    def _():
        acc_ref[...] = part

    @pl.when(j != 0)
    def _():
        acc_ref[...] += part

    @pl.when(j == pl.num_programs(1) - 1)
    def _():
        o_ref[...] = x_ref[...] + acc_ref[...]


def sgu_merge(x, att, rest, g_sgu, w_sgu, b_sgu, w_pa, w_pb, w_o, chunked, tm, tn):
    t, d = x.shape
    a_w = att.shape[1]
    b_w = g_sgu.shape[0]
    tm = min(tm, t)
    assert d % tn == 0 and b_w % tn == 0 and a_w == b_w and tn == b_w
    nj = d // tn
    n_groups = w_sgu.shape[0]
    if chunked:
        assert tm % SGU_CHUNK == 0
        ws = w_sgu
        bs = jnp.broadcast_to(b_sgu[:, :, None], (n_groups, SGU_CHUNK, SGU_GROUP_DIM))
        ws_spec = pl.BlockSpec((n_groups, SGU_CHUNK, SGU_CHUNK), lambda i, j: (0, 0, 0))
        bs_spec = pl.BlockSpec((n_groups, SGU_CHUNK, SGU_GROUP_DIM), lambda i, j: (0, 0, 0))
    else:
        ws = jnp.repeat(w_sgu[:, 0, 0], SGU_GROUP_DIM).reshape(1, b_w)
        bs = jnp.repeat(b_sgu[:, 0], SGU_GROUP_DIM).reshape(1, b_w)
        ws_spec = bs_spec = pl.BlockSpec((1, b_w), lambda i, j: (0, 0))
    u_blk = b_w // tn
    in_specs = [
        pl.BlockSpec((tm, d), lambda i, j: (i, 0)),
        pl.BlockSpec((tm, a_w), lambda i, j: (i, 0)),
        pl.BlockSpec((tm, b_w), lambda i, j: (i, 0)),
        pl.BlockSpec((tm, b_w), lambda i, j: (i, 1)),
        pl.BlockSpec((tm, tn), lambda i, j: (i, 2 * u_blk + j)),
        pl.BlockSpec((tm, tn), lambda i, j: (i, 2 * u_blk + nj + j)),
        pl.BlockSpec((1, b_w), lambda i, j: (0, 0)),
        ws_spec, bs_spec,
        pl.BlockSpec((a_w, tn), lambda i, j: (0, j)),
        pl.BlockSpec((b_w, tn), lambda i, j: (0, j)),
        pl.BlockSpec((tn, d), lambda i, j: (j, 0)),
    ]
    x_spec = pl.BlockSpec((tm, d), lambda i, j: (i, 0))
    if chunked:
        out_shape = jax.ShapeDtypeStruct((t, d), F32)
        out_specs = x_spec
    else:
        out_shape = (jax.ShapeDtypeStruct((t, d), F32), jax.ShapeDtypeStruct((t, b_w), F32))
        out_specs = (x_spec, pl.BlockSpec((tm, b_w), lambda i, j: (i, 0)))
    return pl.pallas_call(
        functools.partial(_sgu_merge_kernel, chunked=chunked),
        out_shape=out_shape,
        grid=(t // tm, nj),
        in_specs=in_specs,
        out_specs=out_specs,
        scratch_shapes=[pltpu.VMEM((tm, b_w), BF16), pltpu.VMEM((tm, d), F32)],
        compiler_params=_params("parallel", "arbitrary"),
        name="sgu_merge",
    )(x, att, rest, rest, rest, rest, g_sgu.reshape(1, b_w), ws, bs, w_pa, w_pb, w_o)


def _moe_route_kernel(x_ref, g_ref, wr_ref, br_ref, xn_ref, comb_ref, *, as_list):
    xn = _rms_rows(x_ref[...], g_ref[...])
    xn_ref[...] = xn.astype(xn_ref.dtype)
    tm = xn.shape[0]
    x_hi = xn.astype(BF16)
    x_lo = (xn - x_hi.astype(F32)).astype(BF16)
    parts = jnp.dot(jnp.concatenate([x_hi, x_lo], axis=0), wr_ref[...], preferred_element_type=F32)
    logits = ((parts[0:tm, 0:LANES] + parts[tm:2 * tm, LANES:2 * LANES])
              + (parts[0:tm, LANES:2 * LANES] + parts[tm:2 * tm, 0:LANES])) + br_ref[...]
    lane = lax.broadcasted_iota(jnp.int32, logits.shape, 1)
    lanef = lane.astype(F32)
    e0 = ROUTER_EXPERT_LANE0
    is_grp = lane < e0
    gl = jnp.where(is_grp, logits, -jnp.inf)
    gmax = jnp.max(gl, axis=1, keepdims=True)
    grp = jnp.min(jnp.where(gl == gmax, lanef, 1e9), axis=1, keepdims=True)
    p_grp = 1.0 / jnp.sum(jnp.where(is_grp, jnp.exp(logits - gmax), 0.0), axis=1, keepdims=True)
    lane_grp = jnp.right_shift(lane - e0, 2).astype(F32)
    in_grp = (lane >= e0) & (lane < e0 + N_EXPERTS) & (lane_grp == grp)
    el = jnp.where(in_grp, logits, -jnp.inf)
    t1 = jnp.max(el, axis=1, keepdims=True)
    i1 = jnp.min(jnp.where(el == t1, lanef, 1e9), axis=1, keepdims=True)
    el2 = jnp.where(lanef == i1, -jnp.inf, el)
    t2 = jnp.max(el2, axis=1, keepdims=True)
    i2 = jnp.min(jnp.where(el2 == t2, lanef, 1e9), axis=1, keepdims=True)
    e2 = jnp.exp(t2 - t1)
    w1 = (1.0 / (1.0 + e2)) * p_grp
    w2 = (e2 / (1.0 + e2)) * p_grp
    if as_list:
        comb_ref[...] = jnp.where(lane == 0, i1 - e0, jnp.where(lane == 1, i2 - e0,
                                  jnp.where(lane == 2, w1, jnp.where(lane == 3, w2, 0.0))))
    else:
        comb_ref[...] = jnp.where(lanef == i1, w1, jnp.where(lanef == i2, w2, 0.0))


def moe_route(x, g_ffn, w_rg, b_rg, w_re, b_re, tm, as_list):
    t, d = x.shape
    tm = min(tm, t)
    assert EXP_PER_GROUP == 4 and w_rg.shape[1] == N_EXP_GROUPS and w_re.shape[1] == N_EXPERTS
    pad = LANES - N_EXP_GROUPS - N_EXPERTS
    wr = jnp.concatenate([w_rg, w_re, jnp.zeros((d, pad), F32)], axis=1)
    wr_hi = wr.astype(BF16)
    wr = jnp.concatenate([wr_hi, (wr - wr_hi.astype(F32)).astype(BF16)], axis=1)
    br = jnp.concatenate([b_rg, b_re, jnp.zeros((pad,), F32)]).reshape(1, LANES)
    return pl.pallas_call(
        functools.partial(_moe_route_kernel, as_list=as_list),
        out_shape=(jax.ShapeDtypeStruct((t, d), F32 if as_list else BF16),
                   jax.ShapeDtypeStruct((t, LANES), F32)),
        grid=(t // tm,),
        in_specs=[
            pl.BlockSpec((tm, d), lambda i: (i, 0)),
            pl.BlockSpec((1, d), lambda i: (0, 0)),
            pl.BlockSpec((d, 2 * LANES), lambda i: (0, 0)),
            pl.BlockSpec((1, LANES), lambda i: (0, 0)),
        ],
        out_specs=(pl.BlockSpec((tm, d), lambda i: (i, 0)), pl.BlockSpec((tm, LANES), lambda i: (i, 0))),
        compiler_params=_params("parallel"),
        name="moe_route",
    )(x, g_ffn.reshape(1, d), wr, br)


def _moe_experts_kernel(xn_ref, comb_ref, x_ref, wg_ref, wu_ref, wd_ref, gf_ref, o_ref, acc_ref):
    e = pl.program_id(1)
    lane = lax.broadcasted_iota(jnp.int32, comb_ref.shape, 1)
    c = jnp.sum(jnp.where(lane == e + ROUTER_EXPERT_LANE0, comb_ref[...], 0.0), axis=1, keepdims=True)
    xn = xn_ref[...]
    g = jnp.dot(xn, wg_ref[...].astype(BF16), preferred_element_type=F32)
    u = jnp.dot(xn, wu_ref[...].astype(BF16), preferred_element_type=F32)
    h = (g * _sigmoid(g)) * u
    part = jnp.dot((h * c).astype(BF16), wd_ref[...].astype(BF16), preferred_element_type=F32)

    @pl.when(e == 0)
    def _():
        acc_ref[...] = part

    @pl.when(e != 0)
    def _():
        acc_ref[...] += part

    @pl.when(e == pl.num_programs(1) - 1)
    def _():
        o_ref[...] = _rms_rows(x_ref[...] + acc_ref[...], gf_ref[...])


def moe_experts(xn, comb, x, w_gate, w_up, w_down, g_final, tm):
    t, d = x.shape
    n_exp, _, f = w_gate.shape
    tm = min(tm, t)
    return pl.pallas_call(
        _moe_experts_kernel,
        out_shape=jax.ShapeDtypeStruct((t, d), F32),
        grid=(t // tm, n_exp),
        in_specs=[
            pl.BlockSpec((tm, d), lambda i, e: (i, 0)),
            pl.BlockSpec((tm, LANES), lambda i, e: (i, 0)),
            pl.BlockSpec((tm, d), lambda i, e: (i, 0)),
            pl.BlockSpec((None, d, f), lambda i, e: (e, 0, 0)),
            pl.BlockSpec((None, d, f), lambda i, e: (e, 0, 0)),
            pl.BlockSpec((None, f, d), lambda i, e: (e, 0, 0)),
            pl.BlockSpec((1, d), lambda i, e: (0, 0)),
        ],
        out_specs=pl.BlockSpec((tm, d), lambda i, e: (i, 0)),
        scratch_shapes=[pltpu.VMEM((tm, d), F32)],
        compiler_params=_params("parallel", "arbitrary"),
        name="moe_experts",
    )(xn, comb, x, w_gate, w_up, w_down, g_final.reshape(1, d))


def _dispatch_plan(e_tok, tm):
    n_pairs = e_tok.size
    n_exp = N_EXPERTS
    i32 = jnp.int32
    ef = e_tok.reshape(-1)
    order = jnp.argsort(ef, stable=True).astype(i32)
    inv = jnp.argsort(order).astype(i32)
    sorted_e = ef[order]
    experts = jnp.arange(n_exp, dtype=i32)
    c_start = jnp.sum((sorted_e[None, :] < experts[:, None]).astype(i32), axis=1)
    counts = jnp.concatenate([c_start[1:], jnp.full((1,), n_pairs, i32)]) - c_start
    padded = ((counts + tm - 1) // tm) * tm
    g_end = jnp.cumsum(padded).astype(i32)
    g_start = g_end - padded
    shift = g_start - c_start
    pair_row = inv + shift[ef]
    n_rows = n_pairs + n_exp * tm
    n_tiles = n_rows // tm
    tile_start = jnp.arange(n_tiles, dtype=i32) * tm
    tile_expert = jnp.minimum(jnp.sum((g_end[None, :] <= tile_start[:, None]).astype(i32), axis=1),
                              n_exp - 1)
    row = jnp.arange(n_rows, dtype=i32)
    s_pos = row - jnp.repeat(shift[tile_expert], tm)
    row_token = order[jnp.clip(s_pos, 0, n_pairs - 1)] // 2
    return row_token.astype(i32), tile_expert.astype(i32), pair_row.astype(i32)


def _row_gather(src_hbm, dst, sem, row_ids, base, n_rows, slot):
    def body(r, carry):
        tok = row_ids[base + r]
        pltpu.make_async_copy(src_hbm.at[pl.ds(tok, 1), :], dst.at[slot, pl.ds(r, 1), :], sem.at[slot]).start()
        return carry
    lax.fori_loop(0, n_rows, body, 0, unroll=8)


def _row_gather_static(src_hbm, dst, sem, row_ids, base, n_rows, slot):
    for r in range(n_rows):
        tok = row_ids[base + r]
        pltpu.make_async_copy(src_hbm.at[pl.ds(tok, 1), :], dst.at[slot, pl.ds(r, 1), :], sem.at[slot]).start()


def _row_gather_wait(src_hbm, dst, sem, n_rows, slot):
    def body(r, carry):
        pltpu.make_async_copy(src_hbm.at[pl.ds(0, 1), :], dst.at[slot, pl.ds(r, 1), :], sem.at[slot]).wait()
        return carry
    lax.fori_loop(0, n_rows, body, 0, unroll=8)


def _moe_sparse_kernel(tok_ref, texp_ref, xn_hbm, wg_ref, wu_ref, wd_ref, o_ref, xbuf, wgb, wub, wdb, sem):
    i = pl.program_id(0)
    n = pl.num_programs(0)
    tm = o_ref.shape[0]
    ahead = GATHER_SLOTS - 1

    @pl.when(i == 0)
    def _():
        for t in range(ahead):
            _row_gather(xn_hbm, xbuf, sem, tok_ref, jnp.minimum(t, n - 1) * tm, tm, t)

    @pl.when((i == 0) | (texp_ref[i] != texp_ref[jnp.maximum(i - 1, 0)]))
    def _():
        wgb[...] = wg_ref[...].astype(BF16)
        wub[...] = wu_ref[...].astype(BF16)
        wdb[...] = wd_ref[...].astype(BF16)

    slot = lax.rem(i, GATHER_SLOTS)
    _row_gather_wait(xn_hbm, xbuf, sem, tm, slot)
    xs = xbuf[slot].astype(BF16)
    g = jnp.dot(xs, wgb[...], preferred_element_type=F32)
    u = jnp.dot(xs, wub[...], preferred_element_type=F32)
    hb = ((g * _sigmoid(g)) * u).astype(BF16)
    o_ref[...] = jnp.dot(hb, wdb[...], preferred_element_type=F32)

    _row_gather_static(xn_hbm, xbuf, sem, tok_ref, jnp.minimum(i + ahead, n - 1) * tm, tm,
                       lax.rem(i + ahead, GATHER_SLOTS))

    @pl.when(i == n - 1)
    def _():
        for t in range(1, GATHER_SLOTS):
            _row_gather_wait(xn_hbm, xbuf, sem, tm, lax.rem(i + t, GATHER_SLOTS))


def moe_sparse_experts(xn, row_token, tile_expert, w_gate, w_up, w_down, tm):
    t, d = xn.shape
    n_exp, _, f = w_gate.shape
    n_rows = row_token.shape[0]
    return pl.pallas_call(
        _moe_sparse_kernel,
        out_shape=jax.ShapeDtypeStruct((n_rows, d), F32),
        grid_spec=pltpu.PrefetchScalarGridSpec(
            num_scalar_prefetch=2,
            grid=(n_rows // tm,),
            in_specs=[
                pl.BlockSpec(memory_space=pl.ANY),
                pl.BlockSpec((None, d, f), lambda i, tok, te: (te[i], 0, 0)),
                pl.BlockSpec((None, d, f), lambda i, tok, te: (te[i], 0, 0)),
                pl.BlockSpec((None, f, d), lambda i, tok, te: (te[i], 0, 0)),
            ],
            out_specs=pl.BlockSpec((tm, d), lambda i, tok, te: (i, 0)),
            scratch_shapes=[pltpu.VMEM((GATHER_SLOTS, tm, d), F32), pltpu.VMEM((d, f), BF16),
                            pltpu.VMEM((d, f), BF16), pltpu.VMEM((f, d), BF16),
                            pltpu.SemaphoreType.DMA((GATHER_SLOTS,))],
        ),
        compiler_params=_params("arbitrary"),
        name="moe_sparse_experts",
    )(row_token, tile_expert, xn, w_gate, w_up, w_down)


def _moe_combine_kernel(row_ref, ys_hbm, x_ref, route_ref, gf_ref, o_ref, ybuf, sem):
    i = pl.program_id(0)
    n = pl.num_programs(0)
    tm = o_ref.shape[0]
    ahead = GATHER_SLOTS - 1

    @pl.when(i == 0)
    def _():
        for t in range(ahead):
            _row_gather(ys_hbm, ybuf, sem, row_ref, jnp.minimum(t, n - 1) * 2 * tm, 2 * tm, t)

    slot = lax.rem(i, GATHER_SLOTS)
    _row_gather_wait(ys_hbm, ybuf, sem, 2 * tm, slot)
    k = TOP_K_IN_GROUP
    route = route_ref[...]
    lane = lax.broadcasted_iota(jnp.int32, route.shape, 1)
    w1 = jnp.sum(jnp.where(lane == k, route, 0.0), axis=1, keepdims=True)
    w2 = jnp.sum(jnp.where(lane == k + 1, route, 0.0), axis=1, keepdims=True)
    y = w1 * ybuf[slot, 0:tm, :] + w2 * ybuf[slot, tm:2 * tm, :]
    o_ref[...] = _rms_rows(x_ref[...] + y, gf_ref[...])

    _row_gather_static(ys_hbm, ybuf, sem, row_ref, jnp.minimum(i + ahead, n - 1) * 2 * tm, 2 * tm,
                       lax.rem(i + ahead, GATHER_SLOTS))

    @pl.when(i == n - 1)
    def _():
        for t in range(1, GATHER_SLOTS):
            _row_gather_wait(ys_hbm, ybuf, sem, 2 * tm, lax.rem(i + t, GATHER_SLOTS))


def moe_combine(ys, pair_row, x, route, g_final, tm):
    t, d = x.shape
    rows = jnp.transpose(pair_row.reshape(t // tm, tm, 2), (0, 2, 1)).reshape(-1)
    return pl.pallas_call(
        _moe_combine_kernel,
        out_shape=jax.ShapeDtypeStruct((t, d), F32),
        grid_spec=pltpu.PrefetchScalarGridSpec(
            num_scalar_prefetch=1,
            grid=(t // tm,),
            in_specs=[
                pl.BlockSpec(memory_space=pl.ANY),
                pl.BlockSpec((tm, d), lambda i, rw: (i, 0)),
                pl.BlockSpec((tm, LANES), lambda i, rw: (i, 0)),
                pl.BlockSpec((1, d), lambda i, rw: (0, 0)),
            ],
            out_specs=pl.BlockSpec((tm, d), lambda i, rw: (i, 0)),
            scratch_shapes=[pltpu.VMEM((GATHER_SLOTS, 2 * tm, d), F32),
                            pltpu.SemaphoreType.DMA((GATHER_SLOTS,))],
        ),
        compiler_params=_params("arbitrary"),
        name="moe_combine",
    )(rows, ys, x, route, g_final.reshape(1, d))


def kernel(x_prompt, x_sample, cache_k, cache_v, page_table, g_mix, w_in, g_sgu, w_sgu, b_sgu, w_proj_a, w_proj_b, w_out, g_ffn, w_router_group, b_router_group, w_router_expert, b_router_expert, w_exp_gate, w_exp_up, w_exp_down, g_final):
    depth = g_mix.shape[0]
    assert depth == 1
    layer = 0
    b_sz, s_len, d = x_prompt.shape
    n_seq, n_new, _ = x_sample.shape
    assert n_new == 1
    _, n_pool, page, n_heads, hd = cache_k.shape
    assert page == PAGE_SIZE and hd == HEAD_DIM
    a_w = n_heads * hd
    b_w = g_sgu.shape[1]
    n_pages = page_table.shape[1]
    past_len = n_pages * PAGE_SIZE
    assert past_len % MOBA_BLOCK == 0 and n_pages // PAGES_PER_BLOCK >= MOBA_TOPK

    w_in_l = w_in[layer]
    w_qkv = w_in_l[:, 0:3 * a_w].astype(BF16)
    w_q, w_k, w_v = (w_qkv[:, c * a_w:(c + 1) * a_w] for c in range(3))
    w_rest = w_in_l[:, 3 * a_w:].astype(BF16)
    w_pa = w_proj_a[layer].astype(BF16)
    w_pb = w_proj_b[layer].astype(BF16)
    w_o = w_out[layer].astype(BF16)
    w_eg, w_eu, w_ed = w_exp_gate[layer], w_exp_up[layer], w_exp_down[layer]

    def dense_tail(x2, att, rest, chunked):
        merged = sgu_merge(x2, att, rest, g_sgu[layer], w_sgu[layer], b_sgu[layer], w_pa, w_pb, w_o,
                           chunked=chunked, tm=512, tn=1024)
        x1, vn = (merged, None) if chunked else merged
        sparse = x1.shape[0] >= MOE_SPARSE_MIN_TOKENS
        xn, route = moe_route(x1, g_ffn[layer], w_router_group[layer], b_router_group[layer],
                              w_router_expert[layer], b_router_expert[layer], tm=256, as_list=sparse)
        if sparse:
            e_tok = route[:, 0:TOP_K_IN_GROUP].astype(jnp.int32)
            row_token, tile_expert, pair_row = _dispatch_plan(e_tok, MOE_ROW_TILE)
            ys = moe_sparse_experts(xn, row_token, tile_expert, w_eg, w_eu, w_ed, tm=MOE_ROW_TILE)
            y = moe_combine(ys, pair_row.reshape(-1, TOP_K_IN_GROUP), x1, route, g_final, tm=MOE_ROW_TILE)
        else:
            y = moe_experts(xn, route, x1, w_eg, w_eu, w_ed, g_final, tm=512)
        return y, vn

    xp = x_prompt.reshape(b_sz * s_len, d)
    g_in = g_mix[layer]
    q_p, k_p, v_p, kb_p, vb_p = norm_qkv(xp, g_in, w_qkv, tm=512)
    (rest_p,) = norm_matmul(xp, g_in, w_rest, (BF16,), tm=1024, tn=1024)
    pt_flat = page_table.reshape(-1)
    att_p, blk_sums = moba_prompt(q_p, kb_p, vb_p, block_means(k_p), b_sz, s_len, BF16,
                                  cache_k, layer, pt_flat)
    y_p, _ = dense_tail(xp, att_p, rest_p, chunked=True)

    xs = x_sample.reshape(n_seq, d)
    (q_s,) = norm_matmul(xs, g_in, w_q, (F32,), tm=n_seq, tn=1024)
    (k_s,) = norm_matmul(xs, g_in, w_k, (F32,), tm=n_seq, tn=1024)
    (v_s,) = norm_matmul(xs, g_in, w_v, (F32,), tm=n_seq, tn=1024)
    (rest_s,) = norm_matmul(xs, g_in, w_rest, (F32,), tm=n_seq, tn=1024)
    blk_sums = blk_sums.reshape(n_seq, n_pages // PAGES_PER_BLOCK, n_heads, hd)
    picks = block_choice(blk_sums, q_s)
    idx = jnp.transpose(picks[:, :MOBA_TOPK, :, 0], (0, 2, 1))
    pages = idx[..., None] * PAGES_PER_BLOCK + jnp.arange(PAGES_PER_BLOCK, dtype=jnp.int32)
    phys = jnp.take_along_axis(page_table[:, None, :], pages.reshape(n_seq, n_heads, -1), axis=2)
    att_s = sample_attend(q_s, k_s, v_s, cache_k, cache_v, layer, phys.reshape(-1), idx.reshape(-1),
                          past_len)
    y_s, vn_s = dense_tail(xs, att_s, rest_s, chunked=False)

    return (
        y_p.reshape(b_sz, s_len, d),
        y_s.reshape(n_seq, n_new, d),
        k_p.reshape(depth, b_sz, s_len, n_heads, hd),
        v_p.reshape(depth, b_sz, s_len, n_heads, hd),
        k_s.reshape(depth, n_seq, n_new, n_heads, hd),
        v_s.reshape(depth, n_seq, n_new, n_heads, hd),
        vn_s.reshape(depth, n_seq, n_new, b_w),
    )
```

```python
import functools

import jax
import jax.numpy as jnp
from jax import lax
from jax.experimental import pallas as pl
from jax.experimental.pallas import tpu as pltpu

NORM_EPS = 1e-6
NEG_INF = -1e30
HEAD_DIM = 128
MOBA_BLOCK = 256
MOBA_TOPK = 3
PAGE_SIZE = 128
PAGES_PER_BLOCK = MOBA_BLOCK // PAGE_SIZE
SGU_CHUNK = 128
SGU_GROUP_DIM = 128
N_EXP_GROUPS = 4
EXP_PER_GROUP = 4
N_EXPERTS = N_EXP_GROUPS * EXP_PER_GROUP
TOP_K_IN_GROUP = 2
MOE_ROW_TILE = 256
MOE_SPARSE_MIN_TOKENS = 2048
GATHER_SLOTS = 3
LANES = 128
ROUTER_EXPERT_LANE0 = N_EXP_GROUPS
VMEM_LIMIT_BYTES = 56 * 1024 * 1024
PAGE_UNIT = 8
MOBA_HEADS_PER_STEP = 8
LOG2E = 1.4426950408889634

F32 = jnp.float32
BF16 = jnp.bfloat16


def _params(*semantics):
    return pltpu.CompilerParams(dimension_semantics=semantics, vmem_limit_bytes=VMEM_LIMIT_BYTES)


def _rms_rows(x, g):
    inv = lax.rsqrt(jnp.mean(x * x, axis=-1, keepdims=True) + NORM_EPS)
    return (x * inv) * g


def _sigmoid(x):
    return 1.0 / (1.0 + jnp.exp(-x))


def _norm_matmul_kernel(x_ref, g_ref, w_ref, *refs, row_chunk):
    out_refs, xn_ref = refs[:-1], refs[-1]

    @pl.when(pl.program_id(1) == 0)
    def _():
        def body(r, carry):
            rows = pl.ds(pl.multiple_of(r * row_chunk, row_chunk), row_chunk)
            xn_ref[rows, :] = _rms_rows(x_ref[rows, :], g_ref[...]).astype(xn_ref.dtype)
            return carry
        lax.fori_loop(0, x_ref.shape[0] // row_chunk, body, 0)

    res = jnp.dot(xn_ref[...], w_ref[...], preferred_element_type=F32)
    for o_ref in out_refs:
        o_ref[...] = res.astype(o_ref.dtype)


def _norm_qkv_kernel(x_ref, g_ref, w_ref, q_ref, k_ref, v_ref, kb_ref, vb_ref, xn_ref, *, row_chunk):
    j = pl.program_id(1)

    @pl.when(j == 0)
    def _():
        def body(r, carry):
            rows = pl.ds(pl.multiple_of(r * row_chunk, row_chunk), row_chunk)
            xn_ref[rows, :] = _rms_rows(x_ref[rows, :], g_ref[...]).astype(xn_ref.dtype)
            return carry
        lax.fori_loop(0, x_ref.shape[0] // row_chunk, body, 0)

    res = jnp.dot(xn_ref[...], w_ref[...], preferred_element_type=F32)

    @pl.when(j == 0)
    def _():
        q_ref[...] = res

    @pl.when(j == 1)
    def _():
        k_ref[...] = res
        kb_ref[...] = res.astype(kb_ref.dtype)

    @pl.when(j == 2)
    def _():
        v_ref[...] = res
        vb_ref[...] = res.astype(vb_ref.dtype)


def norm_qkv(x, g, w_qkv, tm):
    t, d = x.shape
    a_w = w_qkv.shape[1] // 3
    tm = min(tm, t)
    row_chunk = min(tm, 128)
    out_spec = pl.BlockSpec((tm, a_w), lambda i, j: (i, 0))
    return pl.pallas_call(
        functools.partial(_norm_qkv_kernel, row_chunk=row_chunk),
        out_shape=tuple(jax.ShapeDtypeStruct((t, a_w), dt) for dt in (F32, F32, F32, BF16, BF16)),
        grid=(t // tm, 3),
        in_specs=[
            pl.BlockSpec((tm, d), lambda i, j: (i, 0)),
            pl.BlockSpec((1, d), lambda i, j: (0, 0)),
            pl.BlockSpec((d, a_w), lambda i, j: (0, j)),
        ],
        out_specs=tuple(out_spec for _ in range(5)),
        scratch_shapes=[pltpu.VMEM((tm, d), BF16)],
        compiler_params=_params("parallel", "arbitrary"),
        name="norm_qkv",
    )(x, g.reshape(1, d), w_qkv)


def norm_matmul(x, g, w, out_dtypes, tm, tn):
    t, d = x.shape
    n = w.shape[1]
    tm = min(tm, t)
    tn = min(tn, n)
    row_chunk = min(tm, 128)
    out_spec = pl.BlockSpec((tm, tn), lambda i, j: (i, j))
    return pl.pallas_call(
        functools.partial(_norm_matmul_kernel, row_chunk=row_chunk),
        out_shape=tuple(jax.ShapeDtypeStruct((t, n), dt) for dt in out_dtypes),
        grid=(t // tm, n // tn),
        in_specs=[
            pl.BlockSpec((tm, d), lambda i, j: (i, 0)),
            pl.BlockSpec((1, d), lambda i, j: (0, 0)),
            pl.BlockSpec((d, tn), lambda i, j: (0, j)),
        ],
        out_specs=tuple(out_spec for _ in out_dtypes),
        scratch_shapes=[pltpu.VMEM((tm, d), BF16)],
        compiler_params=_params("parallel", "arbitrary"),
        name="norm_matmul",
    )(x, g.reshape(1, d), w)


def _block_means_kernel(k_ref, o_ref):
    blk = MOBA_BLOCK
    for n in range(o_ref.shape[0]):
        o_ref[n:n + 1, :] = jnp.sum(k_ref[n * blk:(n + 1) * blk, :], axis=0, keepdims=True) * (1.0 / blk)


def block_means(k):
    t, width = k.shape
    per_step = 8
    rows = per_step * MOBA_BLOCK
    assert t % rows == 0
    return pl.pallas_call(
        _block_means_kernel,
        out_shape=jax.ShapeDtypeStruct((t // MOBA_BLOCK, width), F32),
        grid=(t // rows,),
        in_specs=[pl.BlockSpec((rows, width), lambda i: (i, 0))],
        out_specs=pl.BlockSpec((per_step, width), lambda i: (i, 0)),
        compiler_params=_params("parallel"),
        name="block_means",
    )(k)


def _moba_prompt_kernel(pt_ref, q_ref, k_ref, v_ref, kmean_ref, ck_hbm, o_ref, sums_hbm,
                        e_ref, bias_ref, biasd_ref, gate_ref, pbuf, stage, psem, osem,
                        *, n_blk, n_heads, hp, layer):
    blk = MOBA_BLOCK
    hd = HEAD_DIM
    bat = pl.program_id(0)
    grp = pl.program_id(1)
    qi = pl.program_id(2)
    n_grp = pl.num_programs(1)
    c2 = hd ** -0.5 * LOG2E
    nt = (((1,), (1,)), ((), ()))
    slopes = [jnp.exp2(jnp.full((1, 1), -8.0 / n_heads, F32) * (grp * hp + t + 1).astype(F32)) * LOG2E
              for t in range(hp)]

    pu = PAGE_UNIT
    units_used = pt_ref.shape[0] // pu
    sums_per_unit = pu // PAGES_PER_BLOCK
    unit0 = (bat * n_grp + grp) * (n_blk * (n_blk + 1) // 2) + (qi * (qi + 1)) // 2

    def page_copies(u, slot):
        return [pltpu.make_async_copy(ck_hbm.at[layer, pt_ref[u * pu + r]], pbuf.at[slot, r], psem.at[slot])
                for r in range(pu)]

    def sums_copy(u, slot):
        return pltpu.make_async_copy(stage.at[slot], sums_hbm.at[pl.ds(u * sums_per_unit, sums_per_unit)],
                                     osem.at[slot])

    def page_unit(u):
        slot = lax.rem(u, 2)

        @pl.when((u >= 2) & (u - 2 < units_used))
        def _():
            sums_copy(u - 2, slot).wait()

        @pl.when(u < units_used)
        def _():
            @pl.when(u == 0)
            def _():
                for cp in page_copies(u, slot):
                    cp.start()

            @pl.when(u + 1 < units_used)
            def _():
                for cp in page_copies(u + 1, 1 - slot):
                    cp.start()

            for cp in page_copies(u, slot):
                cp.wait()
            for i in range(sums_per_unit):
                stage[slot, i] = sum(jnp.sum(pbuf[slot, PAGES_PER_BLOCK * i + r], axis=0)
                                     for r in range(PAGES_PER_BLOCK))
            sums_copy(u, slot).start()

    page_unit(unit0)

    @pl.when(qi == 0)
    def _():
        lane = lax.broadcasted_iota(jnp.int32, (blk, hd), 1)
        for n in range(n_blk):
            e_ref[n * blk:(n + 1) * blk, :] = jnp.where(lane == n, 1.0, 0.0).astype(BF16)
        r = lax.broadcasted_iota(jnp.int32, (blk, blk), 0)
        c = lax.broadcasted_iota(jnp.int32, (blk, blk), 1)
        for t in range(hp):
            alibi = -slopes[t] * (r - c).astype(F32)
            bias_ref[t] = alibi
            biasd_ref[t] = jnp.where(c <= r, alibi, NEG_INF)

    own = pl.ds(pl.multiple_of(qi * blk, blk), blk)
    ones = jnp.ones((blk, hd), BF16)
    q_augs, carry0 = [], []
    for t in range(hp):
        cols = slice(t * hd, (t + 1) * hd)
        q = q_ref[:, cols]
        qb = q.astype(BF16)
        gate_ref[t, 0:n_blk, :] = lax.dot_general(kmean_ref[:, cols], q, nt, preferred_element_type=F32,
                                                  precision=lax.Precision.HIGHEST)
        g_all = gate_ref[t, 0:n_blk, :]
        row = lax.broadcasted_iota(jnp.int32, g_all.shape, 0)
        rank = jnp.zeros(g_all.shape, jnp.int32)
        for j in range(n_blk):
            gj = gate_ref[t, j:j + 1, :]
            ahead = (gj > g_all) | ((gj == g_all) & (j < row))
            rank = rank + jnp.where(ahead & (j < qi), 1, 0)
        chosen = (row < qi) & (rank < MOBA_TOPK)
        gate_ref[t] = jnp.zeros(gate_ref.shape[1:], F32)
        gate_ref[t, 0:n_blk, :] = jnp.where(chosen, 0.0, NEG_INF)
        q_augs.append(jnp.concatenate([qb, gate_ref[t].T.astype(BF16)], axis=1))

        s = lax.dot_general(qb, k_ref[own, cols], nt, preferred_element_type=F32) * c2 + biasd_ref[t]
        m0 = jnp.max(s, axis=1, keepdims=True)
        p = jnp.exp2((s - m0).astype(BF16))
        v_aug = jnp.concatenate([v_ref[own, cols], ones], axis=1)
        carry0 += [m0, jnp.dot(p, v_aug, preferred_element_type=F32)]

    def body(j, carry):
        page_unit(unit0 + 1 + j)
        rows = pl.ds(pl.multiple_of(j * blk, blk), blk)
        dist = (blk * (qi - j)).astype(F32)
        e_j = e_ref[rows, :]
        out = []
        for t in range(hp):
            m, acc = carry[2 * t:2 * t + 2]
            cols = slice(t * hd, (t + 1) * hd)
            k_aug = jnp.concatenate([k_ref[rows, cols], e_j], axis=1)
            s = lax.dot_general(q_augs[t], k_aug, nt, preferred_element_type=F32) * c2 + bias_ref[t]
            off = -slopes[t] * dist
            m_new = jnp.maximum(m, jnp.max(s, axis=1, keepdims=True) + off)
            alpha = jnp.exp2(m - m_new)
            p = jnp.exp2((s - (m_new - off)).astype(BF16))
            v_aug = jnp.concatenate([v_ref[rows, cols], ones], axis=1)
            acc = alpha * acc + jnp.dot(p, v_aug, preferred_element_type=F32)
            out += [m_new, acc]
        return tuple(out)

    final = lax.fori_loop(0, qi, body, tuple(carry0))
    for t in range(hp):
        acc = final[2 * t + 1]
        o_ref[:, t * hd:(t + 1) * hd] = (acc[:, 0:hd] / acc[:, hd:2 * hd]).astype(o_ref.dtype)


def moba_prompt(q, k_bf, v_bf, kmeans, b_sz, s_len, out_dtype, cache_k, layer, page_table_flat):
    t, width = q.shape
    hd = HEAD_DIM
    n_heads = width // hd
    hp = MOBA_HEADS_PER_STEP
    blk = MOBA_BLOCK
    n_blk = s_len // blk
    n_grp = n_heads // hp
    assert s_len % blk == 0 and n_blk % 8 == 0 and n_blk <= LANES and n_heads % hp == 0
    n_pages = page_table_flat.shape[0]
    pu = PAGE_UNIT
    units = b_sz * n_grp * (n_blk * (n_blk + 1) // 2)
    assert n_pages % pu == 0 and pu % PAGES_PER_BLOCK == 0 and units >= n_pages // pu + 2
    page_shape = cache_k.shape[2:]
    whole_seq = pl.BlockSpec((s_len, hp * hd), lambda b, g, i, pt: (b, g), pipeline_mode=pl.Buffered(1))
    return pl.pallas_call(
        functools.partial(_moba_prompt_kernel, n_blk=n_blk, n_heads=n_heads, hp=hp, layer=layer),
        out_shape=(jax.ShapeDtypeStruct((t, width), out_dtype),
                   jax.ShapeDtypeStruct((n_pages // PAGES_PER_BLOCK,) + page_shape[1:], F32)),
        grid_spec=pltpu.PrefetchScalarGridSpec(
            num_scalar_prefetch=1,
            grid=(b_sz, n_grp, n_blk),
            in_specs=[
                pl.BlockSpec((blk, hp * hd), lambda b, g, i, pt: (b * n_blk + i, g)),
                whole_seq,
                whole_seq,
                pl.BlockSpec((n_blk, hp * hd), lambda b, g, i, pt: (b, g)),
                pl.BlockSpec(memory_space=pl.ANY),
            ],
            out_specs=(pl.BlockSpec((blk, hp * hd), lambda b, g, i, pt: (b * n_blk + i, g)),
                       pl.BlockSpec(memory_space=pl.ANY)),
            scratch_shapes=[
                pltpu.VMEM((s_len, hd), BF16),
                pltpu.VMEM((hp, blk, blk), F32),
                pltpu.VMEM((hp, blk, blk), F32),
                pltpu.VMEM((hp, LANES, blk), F32),
                pltpu.VMEM((2, pu) + page_shape, F32),
                pltpu.VMEM((2, pu // PAGES_PER_BLOCK) + page_shape[1:], F32),
                pltpu.SemaphoreType.DMA((2,)),
                pltpu.SemaphoreType.DMA((2,)),
            ],
        ),
        compiler_params=_params("arbitrary", "arbitrary", "arbitrary"),
        name="moba_prompt",
    )(page_table_flat, q, k_bf, v_bf, kmeans, cache_k)


def _block_choice_kernel(bs_ref, q_ref, o_ref):
    bs = bs_ref[...]
    gate = jnp.broadcast_to(jnp.sum(bs * q_ref[...], axis=2, keepdims=True), bs.shape)
    blk_id = lax.broadcasted_iota(jnp.int32, bs.shape, 0).astype(F32)
    o_ref[...] = jnp.zeros_like(o_ref)
    for s in range(MOBA_TOPK):
        mx = jnp.max(gate, axis=0, keepdims=True)
        idx = jnp.min(jnp.where(gate == mx, blk_id, 1e9), axis=0, keepdims=True)
        o_ref[s] = idx[0].astype(jnp.int32)
        gate = jnp.where(blk_id == idx, -jnp.inf, gate)


def block_choice(blk_sums, q_s):
    n_seq, n_blk, n_heads, hd = blk_sums.shape
    return pl.pallas_call(
        _block_choice_kernel,
        out_shape=jax.ShapeDtypeStruct((n_seq, 8, n_heads, hd), jnp.int32),
        grid=(n_seq,),
        in_specs=[
            pl.BlockSpec((None, n_blk, n_heads, hd), lambda b: (b, 0, 0, 0)),
            pl.BlockSpec((None, n_heads, hd), lambda b: (b, 0, 0)),
        ],
        out_specs=pl.BlockSpec((None, 8, n_heads, hd), lambda b: (b, 0, 0, 0)),
        compiler_params=_params("parallel"),
        name="block_choice",
    )(blk_sums, q_s.reshape(n_seq, n_heads, hd))


def _sample_attend_kernel(phys_ref, idx_ref, q_ref, kn_ref, vn_ref, ck_ref, cv_ref, o_ref,
                          kbuf, vbuf, sem, *, n_heads, past_len, layer):
    n_sel = MOBA_TOPK * PAGES_PER_BLOCK
    n_keys = n_sel * PAGE_SIZE
    b = pl.program_id(0)
    nb = pl.num_programs(0)
    hd = HEAD_DIM
    scale = hd ** -0.5
    slot = lax.rem(b, 2)

    def copies(seq, slot_):
        out = []
        for h in range(n_heads):
            for s in range(n_sel):
                page = phys_ref[(seq * n_heads + h) * n_sel + s]
                rows = pl.ds(s * PAGE_SIZE, PAGE_SIZE)
                out.append(pltpu.make_async_copy(ck_ref.at[layer, page, :, h, :],
                                                 kbuf.at[slot_, h, rows, :], sem.at[0, slot_]))
                out.append(pltpu.make_async_copy(cv_ref.at[layer, page, :, h, :],
                                                 vbuf.at[slot_, h, rows, :], sem.at[1, slot_]))
        return out

    @pl.when(b == 0)
    def _():
        for n, cp in enumerate(copies(b, slot)):
            cp.start(priority=n % 2)

    @pl.when(b + 1 < nb)
    def _():
        for n, cp in enumerate(copies(b + 1, 1 - slot)):
            cp.start(priority=n % 2)

    for cp in copies(b, slot):
        cp.wait()

    key_row = lax.broadcasted_iota(jnp.int32, (PAGE_SIZE, hd), 0)
    for h in range(n_heads):
        slope = 2.0 ** (-8.0 * (h + 1) / n_heads)
        q = q_ref[h:h + 1, :]
        qk = lax.dot_general(kbuf[slot, h], jnp.broadcast_to(q, (hd, hd)), (((1,), (1,)), ((), ())),
                             preferred_element_type=F32, precision=lax.Precision.HIGHEST)
        dists = []
        for s in range(n_sel):
            blk = idx_ref[(b * n_heads + h) * MOBA_TOPK + s // PAGES_PER_BLOCK]
            pos0 = blk * MOBA_BLOCK + (s % PAGES_PER_BLOCK) * PAGE_SIZE
            dists.append((past_len - pos0 - key_row).astype(F32))
        sc = qk * scale - slope * jnp.concatenate(dists, axis=0)
        s_self = jnp.sum(q * kn_ref[h:h + 1, :], axis=1, keepdims=True) * scale
        m = jnp.maximum(jnp.max(sc, axis=0, keepdims=True), s_self)
        p = jnp.exp(sc - m)
        p_self = jnp.exp(s_self - m)
        l = jnp.sum(p, axis=0, keepdims=True) + p_self
        acc = jnp.sum(p * vbuf[slot, h], axis=0, keepdims=True) + p_self * vn_ref[h:h + 1, :]
        o_ref[h:h + 1, :] = acc / l


def sample_attend(q_s, k_s, v_s, cache_k, cache_v, layer, phys, idx_flat, past_len):
    n_seq, width = q_s.shape
    hd = HEAD_DIM
    n_heads = width // hd
    n_keys = MOBA_TOPK * MOBA_BLOCK
    row_spec = pl.BlockSpec((None, n_heads, hd), lambda b, ph, ix: (b, 0, 0))
    to3 = lambda a: a.reshape(n_seq, n_heads, hd)
    out = pl.pallas_call(
        functools.partial(_sample_attend_kernel, n_heads=n_heads, past_len=past_len, layer=layer),
        out_shape=jax.ShapeDtypeStruct((n_seq, n_heads, hd), F32),
        grid_spec=pltpu.PrefetchScalarGridSpec(
            num_scalar_prefetch=2,
            grid=(n_seq,),
            in_specs=[row_spec, row_spec, row_spec,
                      pl.BlockSpec(memory_space=pl.ANY), pl.BlockSpec(memory_space=pl.ANY)],
            out_specs=row_spec,
            scratch_shapes=[
                pltpu.VMEM((2, n_heads, n_keys, hd), F32),
                pltpu.VMEM((2, n_heads, n_keys, hd), F32),
                pltpu.SemaphoreType.DMA((2, 2)),
            ],
        ),
        compiler_params=_params("arbitrary"),
        name="sample_attend",
    )(phys, idx_flat, to3(q_s), to3(k_s), to3(v_s), cache_k, cache_v)
    return out.reshape(n_seq, width)


def _sgu_merge_kernel(x_ref, att_ref, u_ref, vg_ref, ga_ref, gb_ref, gs_ref, ws_ref, bs_ref,
                      pa_ref, pb_ref, wo_ref, *refs, chunked):
    if chunked:
        o_ref, m_ref, acc_ref = refs
    else:
        o_ref, vn_out_ref, m_ref, acc_ref = refs
    j = pl.program_id(1)
    tm = x_ref.shape[0]

    @pl.when(j == 0)
    def _():
        vn = _rms_rows(vg_ref[...].astype(F32), gs_ref[...])
        if chunked:
            ch = SGU_CHUNK
            gd = SGU_GROUP_DIM
            r = lax.broadcasted_iota(jnp.int32, (ch, ch), 0)
            c = lax.broadcasted_iota(jnp.int32, (ch, ch), 1)
            vnb = vn.astype(BF16)
            for g in range(ws_ref.shape[0]):
                w = jnp.where(c <= r, ws_ref[g], 0.0).astype(BF16)
                bias = bs_ref[g]
                for cc in range(tm // ch):
                    rows = slice(cc * ch, (cc + 1) * ch)
                    cols = slice(g * gd, (g + 1) * gd)
                    z = jnp.dot(w, vnb[rows, cols], preferred_element_type=F32) + bias
                    m_ref[rows, cols] = (u_ref[rows, cols].astype(F32) * z).astype(m_ref.dtype)
        else:
            vn_out_ref[...] = vn
            z = vn * ws_ref[...] + bs_ref[...]
            m_ref[...] = (u_ref[...].astype(F32) * z).astype(m_ref.dtype)

    pa = jnp.dot(att_ref[...].astype(BF16), pa_ref[...], preferred_element_type=F32)
    pb = jnp.dot(m_ref[...], pb_ref[...], preferred_element_type=F32)
    mixed = _sigmoid(ga_ref[...].astype(F32)) * pa + _sigmoid(gb_ref[...].astype(F32)) * pb
    part = jnp.dot(mixed.astype(BF16), wo_ref[...], preferred_element_type=F32)

    @pl.when(j == 0)
    def _():
        acc_ref[...] = part

    @pl.when(j != 0)
    def _():
        acc_ref[...] += part

    @pl.when(j == pl.num_programs(1) - 1)
    def _():
        o_ref[...] = x_ref[...] + acc_ref[...]


def sgu_merge(x, att, rest, g_sgu, w_sgu, b_sgu, w_pa, w_pb, w_o, chunked, tm, tn):
    t, d = x.shape
    a_w = att.shape[1]
    b_w = g_sgu.shape[0]
    tm = min(tm, t)
    assert d % tn == 0 and b_w % tn == 0 and a_w == b_w and tn == b_w
    nj = d // tn
    n_groups = w_sgu.shape[0]
    if chunked:
        assert tm % SGU_CHUNK == 0
        ws = w_sgu
        bs = jnp.broadcast_to(b_sgu[:, :, None], (n_groups, SGU_CHUNK, SGU_GROUP_DIM))
        ws_spec = pl.BlockSpec((n_groups, SGU_CHUNK, SGU_CHUNK), lambda i, j: (0, 0, 0))
        bs_spec = pl.BlockSpec((n_groups, SGU_CHUNK, SGU_GROUP_DIM), lambda i, j: (0, 0, 0))
    else:
        ws = jnp.repeat(w_sgu[:, 0, 0], SGU_GROUP_DIM).reshape(1, b_w)
        bs = jnp.repeat(b_sgu[:, 0], SGU_GROUP_DIM).reshape(1, b_w)
        ws_spec = bs_spec = pl.BlockSpec((1, b_w), lambda i, j: (0, 0))
    u_blk = b_w // tn
    in_specs = [
        pl.BlockSpec((tm, d), lambda i, j: (i, 0)),
        pl.BlockSpec((tm, a_w), lambda i, j: (i, 0)),
        pl.BlockSpec((tm, b_w), lambda i, j: (i, 0)),
        pl.BlockSpec((tm, b_w), lambda i, j: (i, 1)),
        pl.BlockSpec((tm, tn), lambda i, j: (i, 2 * u_blk + j)),
        pl.BlockSpec((tm, tn), lambda i, j: (i, 2 * u_blk + nj + j)),
        pl.BlockSpec((1, b_w), lambda i, j: (0, 0)),
        ws_spec, bs_spec,
        pl.BlockSpec((a_w, tn), lambda i, j: (0, j)),
        pl.BlockSpec((b_w, tn), lambda i, j: (0, j)),
        pl.BlockSpec((tn, d), lambda i, j: (j, 0)),
    ]
    x_spec = pl.BlockSpec((tm, d), lambda i, j: (i, 0))
    if chunked:
        out_shape = jax.ShapeDtypeStruct((t, d), F32)
        out_specs = x_spec
    else:
        out_shape = (jax.ShapeDtypeStruct((t, d), F32), jax.ShapeDtypeStruct((t, b_w), F32))
        out_specs = (x_spec, pl.BlockSpec((tm, b_w), lambda i, j: (i, 0)))
    return pl.pallas_call(
        functools.partial(_sgu_merge_kernel, chunked=chunked),
        out_shape=out_shape,
        grid=(t // tm, nj),
        in_specs=in_specs,
        out_specs=out_specs,
        scratch_shapes=[pltpu.VMEM((tm, b_w), BF16), pltpu.VMEM((tm, d), F32)],
        compiler_params=_params("parallel", "arbitrary"),
        name="sgu_merge",
    )(x, att, rest, rest, rest, rest, g_sgu.reshape(1, b_w), ws, bs, w_pa, w_pb, w_o)


def _moe_route_kernel(x_ref, g_ref, wr_ref, br_ref, xn_ref, comb_ref, *, as_list):
    xn = _rms_rows(x_ref[...], g_ref[...])
    xn_ref[...] = xn.astype(xn_ref.dtype)
    tm = xn.shape[0]
    x_hi = xn.astype(BF16)
    x_lo = (xn - x_hi.astype(F32)).astype(BF16)
    parts = jnp.dot(jnp.concatenate([x_hi, x_lo], axis=0), wr_ref[...], preferred_element_type=F32)
    logits = ((parts[0:tm, 0:LANES] + parts[tm:2 * tm, LANES:2 * LANES])
              + (parts[0:tm, LANES:2 * LANES] + parts[tm:2 * tm, 0:LANES])) + br_ref[...]
    lane = lax.broadcasted_iota(jnp.int32, logits.shape, 1)
    lanef = lane.astype(F32)
    e0 = ROUTER_EXPERT_LANE0
    is_grp = lane < e0
    gl = jnp.where(is_grp, logits, -jnp.inf)
    gmax = jnp.max(gl, axis=1, keepdims=True)
    grp = jnp.min(jnp.where(gl == gmax, lanef, 1e9), axis=1, keepdims=True)
    p_grp = 1.0 / jnp.sum(jnp.where(is_grp, jnp.exp(logits - gmax), 0.0), axis=1, keepdims=True)
    lane_grp = jnp.right_shift(lane - e0, 2).astype(F32)
    in_grp = (lane >= e0) & (lane < e0 + N_EXPERTS) & (lane_grp == grp)
    el = jnp.where(in_grp, logits, -jnp.inf)
    t1 = jnp.max(el, axis=1, keepdims=True)
    i1 = jnp.min(jnp.where(el == t1, lanef, 1e9), axis=1, keepdims=True)
    el2 = jnp.where(lanef == i1, -jnp.inf, el)
    t2 = jnp.max(el2, axis=1, keepdims=True)
    i2 = jnp.min(jnp.where(el2 == t2, lanef, 1e9), axis=1, keepdims=True)
    e2 = jnp.exp(t2 - t1)
    w1 = (1.0 / (1.0 + e2)) * p_grp
    w2 = (e2 / (1.0 + e2)) * p_grp
    if as_list:
        comb_ref[...] = jnp.where(lane == 0, i1 - e0, jnp.where(lane == 1, i2 - e0,
                                  jnp.where(lane == 2, w1, jnp.where(lane == 3, w2, 0.0))))
    else:
        comb_ref[...] = jnp.where(lanef == i1, w1, jnp.where(lanef == i2, w2, 0.0))


def moe_route(x, g_ffn, w_rg, b_rg, w_re, b_re, tm, as_list):
    t, d = x.shape
    tm = min(tm, t)
    assert EXP_PER_GROUP == 4 and w_rg.shape[1] == N_EXP_GROUPS and w_re.shape[1] == N_EXPERTS
    pad = LANES - N_EXP_GROUPS - N_EXPERTS
    wr = jnp.concatenate([w_rg, w_re, jnp.zeros((d, pad), F32)], axis=1)
    wr_hi = wr.astype(BF16)
    wr = jnp.concatenate([wr_hi, (wr - wr_hi.astype(F32)).astype(BF16)], axis=1)
    br = jnp.concatenate([b_rg, b_re, jnp.zeros((pad,), F32)]).reshape(1, LANES)
    return pl.pallas_call(
        functools.partial(_moe_route_kernel, as_list=as_list),
        out_shape=(jax.ShapeDtypeStruct((t, d), F32 if as_list else BF16),
                   jax.ShapeDtypeStruct((t, LANES), F32)),
        grid=(t // tm,),
        in_specs=[
            pl.BlockSpec((tm, d), lambda i: (i, 0)),
            pl.BlockSpec((1, d), lambda i: (0, 0)),
            pl.BlockSpec((d, 2 * LANES), lambda i: (0, 0)),
            pl.BlockSpec((1, LANES), lambda i: (0, 0)),
        ],
        out_specs=(pl.BlockSpec((tm, d), lambda i: (i, 0)), pl.BlockSpec((tm, LANES), lambda i: (i, 0))),
        compiler_params=_params("parallel"),
        name="moe_route",
    )(x, g_ffn.reshape(1, d), wr, br)


def _moe_experts_kernel(xn_ref, comb_ref, x_ref, wg_ref, wu_ref, wd_ref, gf_ref, o_ref, acc_ref):
    e = pl.program_id(1)
    lane = lax.broadcasted_iota(jnp.int32, comb_ref.shape, 1)
    c = jnp.sum(jnp.where(lane == e + ROUTER_EXPERT_LANE0, comb_ref[...], 0.0), axis=1, keepdims=True)
    xn = xn_ref[...]
    g = jnp.dot(xn, wg_ref[...].astype(BF16), preferred_element_type=F32)
    u = jnp.dot(xn, wu_ref[...].astype(BF16), preferred_element_type=F32)
    h = (g * _sigmoid(g)) * u
    part = jnp.dot((h * c).astype(BF16), wd_ref[...].astype(BF16), preferred_element_type=F32)

    @pl.when(e == 0)
    def _():
        acc_ref[...] = part

    @pl.when(e != 0)
    def _():
        acc_ref[...] += part

    @pl.when(e == pl.num_programs(1) - 1)
    def _():
        o_ref[...] = _rms_rows(x_ref[...] + acc_ref[...], gf_ref[...])


def moe_experts(xn, comb, x, w_gate, w_up, w_down, g_final, tm):
    t, d = x.shape
    n_exp, _, f = w_gate.shape
    tm = min(tm, t)
    return pl.pallas_call(
        _moe_experts_kernel,
        out_shape=jax.ShapeDtypeStruct((t, d), F32),
        grid=(t // tm, n_exp),
        in_specs=[
            pl.BlockSpec((tm, d), lambda i, e: (i, 0)),
            pl.BlockSpec((tm, LANES), lambda i, e: (i, 0)),
            pl.BlockSpec((tm, d), lambda i, e: (i, 0)),
            pl.BlockSpec((None, d, f), lambda i, e: (e, 0, 0)),
            pl.BlockSpec((None, d, f), lambda i, e: (e, 0, 0)),
            pl.BlockSpec((None, f, d), lambda i, e: (e, 0, 0)),
            pl.BlockSpec((1, d), lambda i, e: (0, 0)),
        ],
        out_specs=pl.BlockSpec((tm, d), lambda i, e: (i, 0)),
        scratch_shapes=[pltpu.VMEM((tm, d), F32)],
        compiler_params=_params("parallel", "arbitrary"),
        name="moe_experts",
    )(xn, comb, x, w_gate, w_up, w_down, g_final.reshape(1, d))


def _dispatch_plan(e_tok, tm):
    n_pairs = e_tok.size
    n_exp = N_EXPERTS
    i32 = jnp.int32
    ef = e_tok.reshape(-1)
    order = jnp.argsort(ef, stable=True).astype(i32)
    inv = jnp.argsort(order).astype(i32)
    sorted_e = ef[order]
    experts = jnp.arange(n_exp, dtype=i32)
    c_start = jnp.sum((sorted_e[None, :] < experts[:, None]).astype(i32), axis=1)
    counts = jnp.concatenate([c_start[1:], jnp.full((1,), n_pairs, i32)]) - c_start
    padded = ((counts + tm - 1) // tm) * tm
    g_end = jnp.cumsum(padded).astype(i32)
    g_start = g_end - padded
    shift = g_start - c_start
    pair_row = inv + shift[ef]
    n_rows = n_pairs + n_exp * tm
    n_tiles = n_rows // tm
    tile_start = jnp.arange(n_tiles, dtype=i32) * tm
    tile_expert = jnp.minimum(jnp.sum((g_end[None, :] <= tile_start[:, None]).astype(i32), axis=1),
                              n_exp - 1)
    row = jnp.arange(n_rows, dtype=i32)
    s_pos = row - jnp.repeat(shift[tile_expert], tm)
    row_token = order[jnp.clip(s_pos, 0, n_pairs - 1)] // 2
    return row_token.astype(i32), tile_expert.astype(i32), pair_row.astype(i32)


def _row_gather(src_hbm, dst, sem, row_ids, base, n_rows, slot):
    def body(r, carry):
        tok = row_ids[base + r]
        pltpu.make_async_copy(src_hbm.at[pl.ds(tok, 1), :], dst.at[slot, pl.ds(r, 1), :], sem.at[slot]).start()
        return carry
    lax.fori_loop(0, n_rows, body, 0, unroll=8)


def _row_gather_static(src_hbm, dst, sem, row_ids, base, n_rows, slot):
    for r in range(n_rows):
        tok = row_ids[base + r]
        pltpu.make_async_copy(src_hbm.at[pl.ds(tok, 1), :], dst.at[slot, pl.ds(r, 1), :],
                              sem.at[slot]).start(priority=r % 2)


def _row_gather_wait(src_hbm, dst, sem, n_rows, slot):
    def body(r, carry):
        pltpu.make_async_copy(src_hbm.at[pl.ds(0, 1), :], dst.at[slot, pl.ds(r, 1), :], sem.at[slot]).wait()
        return carry
    lax.fori_loop(0, n_rows, body, 0, unroll=8)


def _moe_sparse_kernel(tok_ref, texp_ref, xn_hbm, wg_ref, wu_ref, wd_ref, o_ref, xbuf, wgb, wub, wdb, sem):
    i = pl.program_id(0)
    n = pl.num_programs(0)
    tm = o_ref.shape[0]
    ahead = GATHER_SLOTS - 1

    @pl.when(i == 0)
    def _():
        for t in range(ahead):
            _row_gather(xn_hbm, xbuf, sem, tok_ref, jnp.minimum(t, n - 1) * tm, tm, t)

    @pl.when((i == 0) | (texp_ref[i] != texp_ref[jnp.maximum(i - 1, 0)]))
    def _():
        wgb[...] = wg_ref[...].astype(BF16)
        wub[...] = wu_ref[...].astype(BF16)
        wdb[...] = wd_ref[...].astype(BF16)

    slot = lax.rem(i, GATHER_SLOTS)
    _row_gather_wait(xn_hbm, xbuf, sem, tm, slot)
    xs = xbuf[slot].astype(BF16)
    g = jnp.dot(xs, wgb[...], preferred_element_type=F32)
    u = jnp.dot(xs, wub[...], preferred_element_type=F32)
    hb = ((g * _sigmoid(g)) * u).astype(BF16)
    o_ref[...] = jnp.dot(hb, wdb[...], preferred_element_type=F32)

    _row_gather_static(xn_hbm, xbuf, sem, tok_ref, jnp.minimum(i + ahead, n - 1) * tm, tm,
                       lax.rem(i + ahead, GATHER_SLOTS))

    @pl.when(i == n - 1)
    def _():
        for t in range(1, GATHER_SLOTS):
            _row_gather_wait(xn_hbm, xbuf, sem, tm, lax.rem(i + t, GATHER_SLOTS))


def moe_sparse_experts(xn, row_token, tile_expert, w_gate, w_up, w_down, tm):
    t, d = xn.shape
    n_exp, _, f = w_gate.shape
    n_rows = row_token.shape[0]
    return pl.pallas_call(
        _moe_sparse_kernel,
        out_shape=jax.ShapeDtypeStruct((n_rows, d), F32),
        grid_spec=pltpu.PrefetchScalarGridSpec(
            num_scalar_prefetch=2,
            grid=(n_rows // tm,),
            in_specs=[
                pl.BlockSpec(memory_space=pl.ANY),
                pl.BlockSpec((None, d, f), lambda i, tok, te: (te[i], 0, 0)),
                pl.BlockSpec((None, d, f), lambda i, tok, te: (te[i], 0, 0)),
                pl.BlockSpec((None, f, d), lambda i, tok, te: (te[i], 0, 0)),
            ],
            out_specs=pl.BlockSpec((tm, d), lambda i, tok, te: (i, 0)),
            scratch_shapes=[pltpu.VMEM((GATHER_SLOTS, tm, d), F32), pltpu.VMEM((d, f), BF16),
                            pltpu.VMEM((d, f), BF16), pltpu.VMEM((f, d), BF16),
                            pltpu.SemaphoreType.DMA((GATHER_SLOTS,))],
        ),
        compiler_params=_params("arbitrary"),
        name="moe_sparse_experts",
    )(row_token, tile_expert, xn, w_gate, w_up, w_down)


def _moe_combine_kernel(row_ref, ys_hbm, x_ref, route_ref, gf_ref, o_ref, ybuf, sem):
    i = pl.program_id(0)
    n = pl.num_programs(0)
    tm = o_ref.shape[0]
    ahead = GATHER_SLOTS - 1

    @pl.when(i == 0)
    def _():
        for t in range(ahead):
            _row_gather(ys_hbm, ybuf, sem, row_ref, jnp.minimum(t, n - 1) * 2 * tm, 2 * tm, t)

    slot = lax.rem(i, GATHER_SLOTS)
    _row_gather_wait(ys_hbm, ybuf, sem, 2 * tm, slot)
    k = TOP_K_IN_GROUP
    route = route_ref[...]
    lane = lax.broadcasted_iota(jnp.int32, route.shape, 1)
    w1 = jnp.sum(jnp.where(lane == k, route, 0.0), axis=1, keepdims=True)
    w2 = jnp.sum(jnp.where(lane == k + 1, route, 0.0), axis=1, keepdims=True)
    y = w1 * ybuf[slot, 0:tm, :] + w2 * ybuf[slot, tm:2 * tm, :]
    o_ref[...] = _rms_rows(x_ref[...] + y, gf_ref[...])

    _row_gather_static(ys_hbm, ybuf, sem, row_ref, jnp.minimum(i + ahead, n - 1) * 2 * tm, 2 * tm,
                       lax.rem(i + ahead, GATHER_SLOTS))

    @pl.when(i == n - 1)
    def _():
        for t in range(1, GATHER_SLOTS):
            _row_gather_wait(ys_hbm, ybuf, sem, 2 * tm, lax.rem(i + t, GATHER_SLOTS))


def moe_combine(ys, pair_row, x, route, g_final, tm):
    t, d = x.shape
    rows = jnp.transpose(pair_row.reshape(t // tm, tm, 2), (0, 2, 1)).reshape(-1)
    return pl.pallas_call(
        _moe_combine_kernel,
        out_shape=jax.ShapeDtypeStruct((t, d), F32),
        grid_spec=pltpu.PrefetchScalarGridSpec(
            num_scalar_prefetch=1,
            grid=(t // tm,),
            in_specs=[
                pl.BlockSpec(memory_space=pl.ANY),
                pl.BlockSpec((tm, d), lambda i, rw: (i, 0)),
                pl.BlockSpec((tm, LANES), lambda i, rw: (i, 0)),
                pl.BlockSpec((1, d), lambda i, rw: (0, 0)),
            ],
            out_specs=pl.BlockSpec((tm, d), lambda i, rw: (i, 0)),
            scratch_shapes=[pltpu.VMEM((GATHER_SLOTS, 2 * tm, d), F32),
                            pltpu.SemaphoreType.DMA((GATHER_SLOTS,))],
        ),
        compiler_params=_params("arbitrary"),
        name="moe_combine",
    )(rows, ys, x, route, g_final.reshape(1, d))


def kernel(x_prompt, x_sample, cache_k, cache_v, page_table, g_mix, w_in, g_sgu, w_sgu, b_sgu, w_proj_a, w_proj_b, w_out, g_ffn, w_router_group, b_router_group, w_router_expert, b_router_expert, w_exp_gate, w_exp_up, w_exp_down, g_final):
    depth = g_mix.shape[0]
    assert depth == 1
    layer = 0
    b_sz, s_len, d = x_prompt.shape
    n_seq, n_new, _ = x_sample.shape
    assert n_new == 1
    _, n_pool, page, n_heads, hd = cache_k.shape
    assert page == PAGE_SIZE and hd == HEAD_DIM
    a_w = n_heads * hd
    b_w = g_sgu.shape[1]
    n_pages = page_table.shape[1]
    past_len = n_pages * PAGE_SIZE
    assert past_len % MOBA_BLOCK == 0 and n_pages // PAGES_PER_BLOCK >= MOBA_TOPK

    w_in_l = w_in[layer]
    w_qkv = w_in_l[:, 0:3 * a_w].astype(BF16)
    w_q, w_k, w_v = (w_qkv[:, c * a_w:(c + 1) * a_w] for c in range(3))
    w_rest = w_in_l[:, 3 * a_w:].astype(BF16)
    w_pa = w_proj_a[layer].astype(BF16)
    w_pb = w_proj_b[layer].astype(BF16)
    w_o = w_out[layer].astype(BF16)
    w_eg, w_eu, w_ed = w_exp_gate[layer], w_exp_up[layer], w_exp_down[layer]

    def dense_tail(x2, att, rest, chunked):
        merged = sgu_merge(x2, att, rest, g_sgu[layer], w_sgu[layer], b_sgu[layer], w_pa, w_pb, w_o,
                           chunked=chunked, tm=512, tn=1024)
        x1, vn = (merged, None) if chunked else merged
        sparse = x1.shape[0] >= MOE_SPARSE_MIN_TOKENS
        xn, route = moe_route(x1, g_ffn[layer], w_router_group[layer], b_router_group[layer],
                              w_router_expert[layer], b_router_expert[layer], tm=256, as_list=sparse)
        if sparse:
            e_tok = route[:, 0:TOP_K_IN_GROUP].astype(jnp.int32)
            row_token, tile_expert, pair_row = _dispatch_plan(e_tok, MOE_ROW_TILE)
            ys = moe_sparse_experts(xn, row_token, tile_expert, w_eg, w_eu, w_ed, tm=MOE_ROW_TILE)
            y = moe_combine(ys, pair_row.reshape(-1, TOP_K_IN_GROUP), x1, route, g_final, tm=MOE_ROW_TILE)
        else:
            y = moe_experts(xn, route, x1, w_eg, w_eu, w_ed, g_final, tm=512)
        return y, vn

    xp = x_prompt.reshape(b_sz * s_len, d)
    g_in = g_mix[layer]
    q_p, k_p, v_p, kb_p, vb_p = norm_qkv(xp, g_in, w_qkv, tm=512)
    (rest_p,) = norm_matmul(xp, g_in, w_rest, (BF16,), tm=1024, tn=1024)
    pt_flat = page_table.reshape(-1)
    att_p, blk_sums = moba_prompt(q_p, kb_p, vb_p, block_means(k_p), b_sz, s_len, BF16,
                                  cache_k, layer, pt_flat)
    y_p, _ = dense_tail(xp, att_p, rest_p, chunked=True)

    xs = x_sample.reshape(n_seq, d)
    (q_s,) = norm_matmul(xs, g_in, w_q, (F32,), tm=n_seq, tn=1024)
    (k_s,) = norm_matmul(xs, g_in, w_k, (F32,), tm=n_seq, tn=1024)
    (v_s,) = norm_matmul(xs, g_in, w_v, (F32,), tm=n_seq, tn=1024)
    (rest_s,) = norm_matmul(xs, g_in, w_rest, (F32,), tm=n_seq, tn=1024)
    blk_sums = blk_sums.reshape(n_seq, n_pages // PAGES_PER_BLOCK, n_heads, hd)
    picks = block_choice(blk_sums, q_s)
    idx = jnp.transpose(picks[:, :MOBA_TOPK, :, 0], (0, 2, 1))
    pages = idx[..., None] * PAGES_PER_BLOCK + jnp.arange(PAGES_PER_BLOCK, dtype=jnp.int32)
    phys = jnp.take_along_axis(page_table[:, None, :], pages.reshape(n_seq, n_heads, -1), axis=2)
    att_s = sample_attend(q_s, k_s, v_s, cache_k, cache_v, layer, phys.reshape(-1), idx.reshape(-1),
                          past_len)
    y_s, vn_s = dense_tail(xs, att_s, rest_s, chunked=False)

    return (
        y_p.reshape(b_sz, s_len, d),
        y_s.reshape(n_seq, n_new, d),
        k_p.reshape(depth, b_sz, s_len, n_heads, hd),
        v_p.reshape(depth, b_sz, s_len, n_heads, hd),
        k_s.reshape(depth, n_seq, n_new, n_heads, hd),
        v_s.reshape(depth, n_seq, n_new, n_heads, hd),
        vn_s.reshape(depth, n_seq, n_new, b_w),
    )
```

```python
import functools

import jax
import jax.numpy as jnp
from jax import lax
from jax.experimental import pallas as pl
from jax.experimental.pallas import tpu as pltpu

NORM_EPS = 1e-6
NEG_INF = -1e30
HEAD_DIM = 128
MOBA_BLOCK = 256
MOBA_TOPK = 3
PAGE_SIZE = 128
PAGES_PER_BLOCK = MOBA_BLOCK // PAGE_SIZE
SGU_CHUNK = 128
SGU_GROUP_DIM = 128
N_EXP_GROUPS = 4
EXP_PER_GROUP = 4
N_EXPERTS = N_EXP_GROUPS * EXP_PER_GROUP
TOP_K_IN_GROUP = 2
MOE_ROW_TILE = 256
MOE_SPARSE_MIN_TOKENS = 2048
GATHER_SLOTS = 3
LANES = 128
ROUTER_EXPERT_LANE0 = N_EXP_GROUPS
VMEM_LIMIT_BYTES = 56 * 1024 * 1024
PAGE_UNIT = 8
MOBA_HEADS_PER_STEP = 8
LOG2E = 1.4426950408889634

F32 = jnp.float32
BF16 = jnp.bfloat16


def _params(*semantics):
    return pltpu.CompilerParams(dimension_semantics=semantics, vmem_limit_bytes=VMEM_LIMIT_BYTES)


def _rms_rows(x, g):
    inv = lax.rsqrt(jnp.mean(x * x, axis=-1, keepdims=True) + NORM_EPS)
    return (x * inv) * g


def _sigmoid(x):
    return 1.0 / (1.0 + jnp.exp(-x))


def _norm_matmul_kernel(x_ref, g_ref, w_ref, *refs, row_chunk):
    out_refs, xn_ref = refs[:-1], refs[-1]

    @pl.when(pl.program_id(1) == 0)
    def _():
        def body(r, carry):
            rows = pl.ds(pl.multiple_of(r * row_chunk, row_chunk), row_chunk)
            xn_ref[rows, :] = _rms_rows(x_ref[rows, :], g_ref[...]).astype(xn_ref.dtype)
            return carry
        lax.fori_loop(0, x_ref.shape[0] // row_chunk, body, 0)

    res = jnp.dot(xn_ref[...], w_ref[...], preferred_element_type=F32)
    for o_ref in out_refs:
        o_ref[...] = res.astype(o_ref.dtype)


def _norm_qkv_kernel(x_ref, g_ref, w_ref, q_ref, k_ref, v_ref, kb_ref, vb_ref, xn_ref, *, row_chunk):
    j = pl.program_id(1)

    @pl.when(j == 0)
    def _():
        def body(r, carry):
            rows = pl.ds(pl.multiple_of(r * row_chunk, row_chunk), row_chunk)
            xn_ref[rows, :] = _rms_rows(x_ref[rows, :], g_ref[...]).astype(xn_ref.dtype)
            return carry
        lax.fori_loop(0, x_ref.shape[0] // row_chunk, body, 0)

    res = jnp.dot(xn_ref[...], w_ref[...], preferred_element_type=F32)

    @pl.when(j == 0)
    def _():
        q_ref[...] = res

    @pl.when(j == 1)
    def _():
        k_ref[...] = res
        kb_ref[...] = res.astype(kb_ref.dtype)

    @pl.when(j == 2)
    def _():
        v_ref[...] = res
        vb_ref[...] = res.astype(vb_ref.dtype)


def norm_qkv(x, g, w_qkv, tm):
    t, d = x.shape
    a_w = w_qkv.shape[1] // 3
    tm = min(tm, t)
    row_chunk = min(tm, 128)
    out_spec = pl.BlockSpec((tm, a_w), lambda i, j: (i, 0))
    return pl.pallas_call(
        functools.partial(_norm_qkv_kernel, row_chunk=row_chunk),
        out_shape=tuple(jax.ShapeDtypeStruct((t, a_w), dt) for dt in (F32, F32, F32, BF16, BF16)),
        grid=(t // tm, 3),
        in_specs=[
            pl.BlockSpec((tm, d), lambda i, j: (i, 0)),
            pl.BlockSpec((1, d), lambda i, j: (0, 0)),
            pl.BlockSpec((d, a_w), lambda i, j: (0, j)),
        ],
        out_specs=tuple(out_spec for _ in range(5)),
        scratch_shapes=[pltpu.VMEM((tm, d), BF16)],
        compiler_params=_params("parallel", "arbitrary"),
        name="norm_qkv",
    )(x, g.reshape(1, d), w_qkv)


def norm_matmul(x, g, w, out_dtypes, tm, tn):
    t, d = x.shape
    n = w.shape[1]
    tm = min(tm, t)
    tn = min(tn, n)
    row_chunk = min(tm, 128)
    out_spec = pl.BlockSpec((tm, tn), lambda i, j: (i, j))
    return pl.pallas_call(
        functools.partial(_norm_matmul_kernel, row_chunk=row_chunk),
        out_shape=tuple(jax.ShapeDtypeStruct((t, n), dt) for dt in out_dtypes),
        grid=(t // tm, n // tn),
        in_specs=[
            pl.BlockSpec((tm, d), lambda i, j: (i, 0)),
            pl.BlockSpec((1, d), lambda i, j: (0, 0)),
            pl.BlockSpec((d, tn), lambda i, j: (0, j)),
        ],
        out_specs=tuple(out_spec for _ in out_dtypes),
        scratch_shapes=[pltpu.VMEM((tm, d), BF16)],
        compiler_params=_params("parallel", "arbitrary"),
        name="norm_matmul",
    )(x, g.reshape(1, d), w)


def _block_means_kernel(k_ref, o_ref):
    blk = MOBA_BLOCK
    for n in range(o_ref.shape[0]):
        o_ref[n:n + 1, :] = jnp.sum(k_ref[n * blk:(n + 1) * blk, :], axis=0, keepdims=True) * (1.0 / blk)


def block_means(k):
    t, width = k.shape
    per_step = 8
    rows = per_step * MOBA_BLOCK
    assert t % rows == 0
    return pl.pallas_call(
        _block_means_kernel,
        out_shape=jax.ShapeDtypeStruct((t // MOBA_BLOCK, width), F32),
        grid=(t // rows,),
        in_specs=[pl.BlockSpec((rows, width), lambda i: (i, 0))],
        out_specs=pl.BlockSpec((per_step, width), lambda i: (i, 0)),
        compiler_params=_params("parallel"),
        name="block_means",
    )(k)


def _moba_prompt_kernel(pt_ref, q_ref, k_ref, v_ref, kmean_ref, ck_hbm, o_ref, sums_hbm,
                        e_ref, bias_ref, biasd_ref, gate_ref, pbuf, stage, psem, osem,
                        *, n_blk, n_heads, hp, layer):
    blk = MOBA_BLOCK
    hd = HEAD_DIM
    bat = pl.program_id(0)
    grp = pl.program_id(1)
    qi = pl.program_id(2)
    n_grp = pl.num_programs(1)
    c2 = hd ** -0.5 * LOG2E
    nt = (((1,), (1,)), ((), ()))
    slopes = [jnp.exp2(jnp.full((1, 1), -8.0 / n_heads, F32) * (grp * hp + t + 1).astype(F32)) * LOG2E
              for t in range(hp)]

    pu = PAGE_UNIT
    units_used = pt_ref.shape[0] // pu
    sums_per_unit = pu // PAGES_PER_BLOCK
    unit0 = (bat * n_grp + grp) * (n_blk * (n_blk + 1) // 2) + (qi * (qi + 1)) // 2

    def page_copies(u, slot):
        return [pltpu.make_async_copy(ck_hbm.at[layer, pt_ref[u * pu + r]], pbuf.at[slot, r], psem.at[slot])
                for r in range(pu)]

    def sums_copy(u, slot):
        return pltpu.make_async_copy(stage.at[slot], sums_hbm.at[pl.ds(u * sums_per_unit, sums_per_unit)],
                                     osem.at[slot])

    def page_unit(u):
        slot = lax.rem(u, 2)

        @pl.when((u >= 2) & (u - 2 < units_used))
        def _():
            sums_copy(u - 2, slot).wait()

        @pl.when(u < units_used)
        def _():
            @pl.when(u == 0)
            def _():
                for cp in page_copies(u, slot):
                    cp.start()

            @pl.when(u + 1 < units_used)
            def _():
                for cp in page_copies(u + 1, 1 - slot):
                    cp.start()

            for cp in page_copies(u, slot):
                cp.wait()
            for i in range(sums_per_unit):
                stage[slot, i] = sum(jnp.sum(pbuf[slot, PAGES_PER_BLOCK * i + r], axis=0)
                                     for r in range(PAGES_PER_BLOCK))
            sums_copy(u, slot).start()

    page_unit(unit0)

    @pl.when(qi == 0)
    def _():
        lane = lax.broadcasted_iota(jnp.int32, (blk, hd), 1)
        for n in range(n_blk):
            e_ref[n * blk:(n + 1) * blk, :] = jnp.where(lane == n, 1.0, 0.0).astype(BF16)
        r = lax.broadcasted_iota(jnp.int32, (blk, blk), 0)
        c = lax.broadcasted_iota(jnp.int32, (blk, blk), 1)
        for t in range(hp):
            alibi = -slopes[t] * (r - c).astype(F32)
            bias_ref[t] = alibi
            biasd_ref[t] = jnp.where(c <= r, alibi, NEG_INF)

    own = pl.ds(pl.multiple_of(qi * blk, blk), blk)
    ones = jnp.ones((blk, hd), BF16)
    q_augs, carry0 = [], []
    for t in range(hp):
        cols = slice(t * hd, (t + 1) * hd)
        q = q_ref[:, cols]
        qb = q.astype(BF16)
        gate_ref[t, 0:n_blk, :] = lax.dot_general(kmean_ref[:, cols], q, nt, preferred_element_type=F32,
                                                  precision=lax.Precision.HIGHEST)
        g_all = gate_ref[t, 0:n_blk, :]
        row = lax.broadcasted_iota(jnp.int32, g_all.shape, 0)
        rank = jnp.zeros(g_all.shape, jnp.int32)
        for j in range(n_blk):
            gj = gate_ref[t, j:j + 1, :]
            ahead = (gj > g_all) | ((gj == g_all) & (j < row))
            rank = rank + jnp.where(ahead & (j < qi), 1, 0)
        chosen = (row < qi) & (rank < MOBA_TOPK)
        gate_ref[t] = jnp.zeros(gate_ref.shape[1:], F32)
        gate_ref[t, 0:n_blk, :] = jnp.where(chosen, 0.0, NEG_INF)
        q_augs.append(jnp.concatenate([qb, gate_ref[t].T.astype(BF16)], axis=1))

        s = lax.dot_general(qb, k_ref[own, cols], nt, preferred_element_type=F32) * c2 + biasd_ref[t]
        m0 = jnp.max(s, axis=1, keepdims=True)
        p = jnp.exp2((s - m0).astype(BF16))
        v_aug = jnp.concatenate([v_ref[own, cols], ones], axis=1)
        carry0 += [m0, jnp.dot(p, v_aug, preferred_element_type=F32)]

    def body(j, carry):
        page_unit(unit0 + 1 + j)
        rows = pl.ds(pl.multiple_of(j * blk, blk), blk)
        dist = (blk * (qi - j)).astype(F32)
        e_j = e_ref[rows, :]
        out = []
        for t in range(hp):
            m, acc = carry[2 * t:2 * t + 2]
            cols = slice(t * hd, (t + 1) * hd)
            k_aug = jnp.concatenate([k_ref[rows, cols], e_j], axis=1)
            s = lax.dot_general(q_augs[t], k_aug, nt, preferred_element_type=F32) * c2 + bias_ref[t]
            off = -slopes[t] * dist
            m_new = jnp.maximum(m, jnp.max(s, axis=1, keepdims=True) + off)
            alpha = jnp.exp2(m - m_new)
            p = jnp.exp2((s - (m_new - off)).astype(BF16))
            v_aug = jnp.concatenate([v_ref[rows, cols], ones], axis=1)
            acc = alpha * acc + jnp.dot(p, v_aug, preferred_element_type=F32)
            out += [m_new, acc]
        return tuple(out)

    final = lax.fori_loop(0, qi, body, tuple(carry0))
    for t in range(hp):
        acc = final[2 * t + 1]
        o_ref[:, t * hd:(t + 1) * hd] = (acc[:, 0:hd] / acc[:, hd:2 * hd]).astype(o_ref.dtype)


def moba_prompt(q, k_bf, v_bf, kmeans, b_sz, s_len, out_dtype, cache_k, layer, page_table_flat):
    t, width = q.shape
    hd = HEAD_DIM
    n_heads = width // hd
    hp = MOBA_HEADS_PER_STEP
    blk = MOBA_BLOCK
    n_blk = s_len // blk
    n_grp = n_heads // hp
    assert s_len % blk == 0 and n_blk % 8 == 0 and n_blk <= LANES and n_heads % hp == 0
    n_pages = page_table_flat.shape[0]
    pu = PAGE_UNIT
    units = b_sz * n_grp * (n_blk * (n_blk + 1) // 2)
    assert n_pages % pu == 0 and pu % PAGES_PER_BLOCK == 0 and units >= n_pages // pu + 2
    page_shape = cache_k.shape[2:]
    whole_seq = pl.BlockSpec((s_len, hp * hd), lambda b, g, i, pt: (b, g), pipeline_mode=pl.Buffered(1))
    return pl.pallas_call(
        functools.partial(_moba_prompt_kernel, n_blk=n_blk, n_heads=n_heads, hp=hp, layer=layer),
        out_shape=(jax.ShapeDtypeStruct((t, width), out_dtype),
                   jax.ShapeDtypeStruct((n_pages // PAGES_PER_BLOCK,) + page_shape[1:], F32)),
        grid_spec=pltpu.PrefetchScalarGridSpec(
            num_scalar_prefetch=1,
            grid=(b_sz, n_grp, n_blk),
            in_specs=[
                pl.BlockSpec((blk, hp * hd), lambda b, g, i, pt: (b * n_blk + i, g)),
                whole_seq,
                whole_seq,
                pl.BlockSpec((n_blk, hp * hd), lambda b, g, i, pt: (b, g)),
                pl.BlockSpec(memory_space=pl.ANY),
            ],
            out_specs=(pl.BlockSpec((blk, hp * hd), lambda b, g, i, pt: (b * n_blk + i, g)),
                       pl.BlockSpec(memory_space=pl.ANY)),
            scratch_shapes=[
                pltpu.VMEM((s_len, hd), BF16),
                pltpu.VMEM((hp, blk, blk), F32),
                pltpu.VMEM((hp, blk, blk), F32),
                pltpu.VMEM((hp, LANES, blk), F32),
                pltpu.VMEM((2, pu) + page_shape, F32),
                pltpu.VMEM((2, pu // PAGES_PER_BLOCK) + page_shape[1:], F32),
                pltpu.SemaphoreType.DMA((2,)),
                pltpu.SemaphoreType.DMA((2,)),
            ],
        ),
        compiler_params=_params("arbitrary", "arbitrary", "arbitrary"),
        name="moba_prompt",
    )(page_table_flat, q, k_bf, v_bf, kmeans, cache_k)


def _block_choice_kernel(bs_ref, q_ref, o_ref):
    bs = bs_ref[...]
    gate = jnp.broadcast_to(jnp.sum(bs * q_ref[...], axis=2, keepdims=True), bs.shape)
    blk_id = lax.broadcasted_iota(jnp.int32, bs.shape, 0).astype(F32)
    o_ref[...] = jnp.zeros_like(o_ref)
    for s in range(MOBA_TOPK):
        mx = jnp.max(gate, axis=0, keepdims=True)
        idx = jnp.min(jnp.where(gate == mx, blk_id, 1e9), axis=0, keepdims=True)
        o_ref[s] = idx[0].astype(jnp.int32)
        gate = jnp.where(blk_id == idx, -jnp.inf, gate)


def block_choice(blk_sums, q_s):
    n_seq, n_blk, n_heads, hd = blk_sums.shape
    return pl.pallas_call(
        _block_choice_kernel,
        out_shape=jax.ShapeDtypeStruct((n_seq, 8, n_heads, hd), jnp.int32),
        grid=(n_seq,),
        in_specs=[
            pl.BlockSpec((None, n_blk, n_heads, hd), lambda b: (b, 0, 0, 0)),
            pl.BlockSpec((None, n_heads, hd), lambda b: (b, 0, 0)),
        ],
        out_specs=pl.BlockSpec((None, 8, n_heads, hd), lambda b: (b, 0, 0, 0)),
        compiler_params=_params("parallel"),
        name="block_choice",
    )(blk_sums, q_s.reshape(n_seq, n_heads, hd))


def _sample_attend_kernel(phys_ref, idx_ref, q_ref, kn_ref, vn_ref, ck_ref, cv_ref, o_ref,
                          kbuf, vbuf, sem, *, n_heads, past_len, layer):
    n_sel = MOBA_TOPK * PAGES_PER_BLOCK
    n_keys = n_sel * PAGE_SIZE
    b = pl.program_id(0)
    nb = pl.num_programs(0)
    hd = HEAD_DIM
    scale = hd ** -0.5
    slot = lax.rem(b, 2)

    def copies(seq, slot_):
        out = []
        for h in range(n_heads):
            for s in range(n_sel):
                page = phys_ref[(seq * n_heads + h) * n_sel + s]
                rows = pl.ds(s * PAGE_SIZE, PAGE_SIZE)
                out.append(pltpu.make_async_copy(ck_ref.at[layer, page, :, h, :],
                                                 kbuf.at[slot_, h, rows, :], sem.at[0, slot_]))
                out.append(pltpu.make_async_copy(cv_ref.at[layer, page, :, h, :],
                                                 vbuf.at[slot_, h, rows, :], sem.at[1, slot_]))
        return out

    @pl.when(b == 0)
    def _():
        for cp in copies(b, slot):
            cp.start()

    @pl.when(b + 1 < nb)
    def _():
        for cp in copies(b + 1, 1 - slot):
            cp.start()

    for cp in copies(b, slot):
        cp.wait()

    key_row = lax.broadcasted_iota(jnp.int32, (PAGE_SIZE, hd), 0)
    for h in range(n_heads):
        slope = 2.0 ** (-8.0 * (h + 1) / n_heads)
        q = q_ref[h:h + 1, :]
        qk = lax.dot_general(kbuf[slot, h], jnp.broadcast_to(q, (hd, hd)), (((1,), (1,)), ((), ())),
                             preferred_element_type=F32, precision=lax.Precision.HIGHEST)
        dists = []
        for s in range(n_sel):
            blk = idx_ref[(b * n_heads + h) * MOBA_TOPK + s // PAGES_PER_BLOCK]
            pos0 = blk * MOBA_BLOCK + (s % PAGES_PER_BLOCK) * PAGE_SIZE
            dists.append((past_len - pos0 - key_row).astype(F32))
        sc = qk * scale - slope * jnp.concatenate(dists, axis=0)
        s_self = jnp.sum(q * kn_ref[h:h + 1, :], axis=1, keepdims=True) * scale
        m = jnp.maximum(jnp.max(sc, axis=0, keepdims=True), s_self)
        p = jnp.exp(sc - m)
        p_self = jnp.exp(s_self - m)
        l = jnp.sum(p, axis=0, keepdims=True) + p_self
        acc = jnp.sum(p * vbuf[slot, h], axis=0, keepdims=True) + p_self * vn_ref[h:h + 1, :]
        o_ref[h:h + 1, :] = acc / l


def sample_attend(q_s, k_s, v_s, cache_k, cache_v, layer, phys, idx_flat, past_len):
    n_seq, width = q_s.shape
    hd = HEAD_DIM
    n_heads = width // hd
    n_keys = MOBA_TOPK * MOBA_BLOCK
    row_spec = pl.BlockSpec((None, n_heads, hd), lambda b, ph, ix: (b, 0, 0))
    to3 = lambda a: a.reshape(n_seq, n_heads, hd)
    out = pl.pallas_call(
        functools.partial(_sample_attend_kernel, n_heads=n_heads, past_len=past_len, layer=layer),
        out_shape=jax.ShapeDtypeStruct((n_seq, n_heads, hd), F32),
        grid_spec=pltpu.PrefetchScalarGridSpec(
            num_scalar_prefetch=2,
            grid=(n_seq,),
            in_specs=[row_spec, row_spec, row_spec,
                      pl.BlockSpec(memory_space=pl.ANY), pl.BlockSpec(memory_space=pl.ANY)],
            out_specs=row_spec,
            scratch_shapes=[
                pltpu.VMEM((2, n_heads, n_keys, hd), F32),
                pltpu.VMEM((2, n_heads, n_keys, hd), F32),
                pltpu.SemaphoreType.DMA((2, 2)),
            ],
        ),
        compiler_params=_params("arbitrary"),
        name="sample_attend",
    )(phys, idx_flat, to3(q_s), to3(k_s), to3(v_s), cache_k, cache_v)
    return out.reshape(n_seq, width)


def _sgu_merge_kernel(x_ref, att_ref, u_ref, vg_ref, ga_ref, gb_ref, gs_ref, ws_ref, bs_ref,
                      pa_ref, pb_ref, wo_ref, *refs, chunked):
    if chunked:
        o_ref, m_ref, acc_ref = refs
    else:
        o_ref, vn_out_ref, m_ref, acc_ref = refs
    j = pl.program_id(1)
    tm = x_ref.shape[0]

    @pl.when(j == 0)
    def _():
        vn = _rms_rows(vg_ref[...].astype(F32), gs_ref[...])
        if chunked:
            ch = SGU_CHUNK
            gd = SGU_GROUP_DIM
            r = lax.broadcasted_iota(jnp.int32, (ch, ch), 0)
            c = lax.broadcasted_iota(jnp.int32, (ch, ch), 1)
            vnb = vn.astype(BF16)
            for g in range(ws_ref.shape[0]):
                w = jnp.where(c <= r, ws_ref[g], 0.0).astype(BF16)
                bias = bs_ref[g]
                for cc in range(tm // ch):
                    rows = slice(cc * ch, (cc + 1) * ch)
                    cols = slice(g * gd, (g + 1) * gd)
                    z = jnp.dot(w, vnb[rows, cols], preferred_element_type=F32) + bias
                    m_ref[rows, cols] = (u_ref[rows, cols].astype(F32) * z).astype(m_ref.dtype)
        else:
            vn_out_ref[...] = vn
            z = vn * ws_ref[...] + bs_ref[...]
            m_ref[...] = (u_ref[...].astype(F32) * z).astype(m_ref.dtype)

    pa = jnp.dot(att_ref[...].astype(BF16), pa_ref[...], preferred_element_type=F32)
    pb = jnp.dot(m_ref[...], pb_ref[...], preferred_element_type=F32)
    mixed = _sigmoid(ga_ref[...].astype(F32)) * pa + _sigmoid(gb_ref[...].astype(F32)) * pb
    part = jnp.dot(mixed.astype(BF16), wo_ref[...], preferred_element_type=F32)

    @pl.when(j == 0)
    def _():
        acc_ref[...] = part

    @pl.when(j != 0)
    def _():
        acc_ref[...] += part

    @pl.when(j == pl.num_programs(1) - 1)
    def _():
        o_ref[...] = x_ref[...] + acc_ref[...]


def sgu_merge(x, att, rest, g_sgu, w_sgu, b_sgu, w_pa, w_pb, w_o, chunked, tm, tn):
    t, d = x.shape
    a_w = att.shape[1]
    b_w = g_sgu.shape[0]
    tm = min(tm, t)
    assert d % tn == 0 and b_w % tn == 0 and a_w == b_w and tn == b_w
    nj = d // tn
    n_groups = w_sgu.shape[0]
    if chunked:
        assert tm % SGU_CHUNK == 0
        ws = w_sgu
        bs = jnp.broadcast_to(b_sgu[:, :, None], (n_groups, SGU_CHUNK, SGU_GROUP_DIM))
        ws_spec = pl.BlockSpec((n_groups, SGU_CHUNK, SGU_CHUNK), lambda i, j: (0, 0, 0))
        bs_spec = pl.BlockSpec((n_groups, SGU_CHUNK, SGU_GROUP_DIM), lambda i, j: (0, 0, 0))
    else:
        ws = jnp.repeat(w_sgu[:, 0, 0], SGU_GROUP_DIM).reshape(1, b_w)
        bs = jnp.repeat(b_sgu[:, 0], SGU_GROUP_DIM).reshape(1, b_w)
        ws_spec = bs_spec = pl.BlockSpec((1, b_w), lambda i, j: (0, 0))
    u_blk = b_w // tn
    in_specs = [
        pl.BlockSpec((tm, d), lambda i, j: (i, 0)),
        pl.BlockSpec((tm, a_w), lambda i, j: (i, 0)),
        pl.BlockSpec((tm, b_w), lambda i, j: (i, 0)),
        pl.BlockSpec((tm, b_w), lambda i, j: (i, 1)),
        pl.BlockSpec((tm, tn), lambda i, j: (i, 2 * u_blk + j)),
        pl.BlockSpec((tm, tn), lambda i, j: (i, 2 * u_blk + nj + j)),
        pl.BlockSpec((1, b_w), lambda i, j: (0, 0)),
        ws_spec, bs_spec,
        pl.BlockSpec((a_w, tn), lambda i, j: (0, j)),
        pl.BlockSpec((b_w, tn), lambda i, j: (0, j)),
        pl.BlockSpec((tn, d), lambda i, j: (j, 0)),
    ]
    x_spec = pl.BlockSpec((tm, d), lambda i, j: (i, 0))
    if chunked:
        out_shape = jax.ShapeDtypeStruct((t, d), F32)
        out_specs = x_spec
    else:
        out_shape = (jax.ShapeDtypeStruct((t, d), F32), jax.ShapeDtypeStruct((t, b_w), F32))
        out_specs = (x_spec, pl.BlockSpec((tm, b_w), lambda i, j: (i, 0)))
    return pl.pallas_call(
        functools.partial(_sgu_merge_kernel, chunked=chunked),
        out_shape=out_shape,
        grid=(t // tm, nj),
        in_specs=in_specs,
        out_specs=out_specs,
        scratch_shapes=[pltpu.VMEM((tm, b_w), BF16), pltpu.VMEM((tm, d), F32)],
        compiler_params=_params("parallel", "arbitrary"),
        name="sgu_merge",
    )(x, att, rest, rest, rest, rest, g_sgu.reshape(1, b_w), ws, bs, w_pa, w_pb, w_o)


def _moe_route_kernel(x_ref, g_ref, wr_ref, br_ref, xn_ref, comb_ref, *, as_list):
    xn = _rms_rows(x_ref[...], g_ref[...])
    xn_ref[...] = xn.astype(xn_ref.dtype)
    tm = xn.shape[0]
    x_hi = xn.astype(BF16)
    x_lo = (xn - x_hi.astype(F32)).astype(BF16)
    parts = jnp.dot(jnp.concatenate([x_hi, x_lo], axis=0), wr_ref[...], preferred_element_type=F32)
    logits = ((parts[0:tm, 0:LANES] + parts[tm:2 * tm, LANES:2 * LANES])
              + (parts[0:tm, LANES:2 * LANES] + parts[tm:2 * tm, 0:LANES])) + br_ref[...]
    lane = lax.broadcasted_iota(jnp.int32, logits.shape, 1)
    lanef = lane.astype(F32)
    e0 = ROUTER_EXPERT_LANE0
    is_grp = lane < e0
    gl = jnp.where(is_grp, logits, -jnp.inf)
    gmax = jnp.max(gl, axis=1, keepdims=True)
    grp = jnp.min(jnp.where(gl == gmax, lanef, 1e9), axis=1, keepdims=True)
    p_grp = 1.0 / jnp.sum(jnp.where(is_grp, jnp.exp(logits - gmax), 0.0), axis=1, keepdims=True)
    lane_grp = jnp.right_shift(lane - e0, 2).astype(F32)
    in_grp = (lane >= e0) & (lane < e0 + N_EXPERTS) & (lane_grp == grp)
    el = jnp.where(in_grp, logits, -jnp.inf)
    t1 = jnp.max(el, axis=1, keepdims=True)
    i1 = jnp.min(jnp.where(el == t1, lanef, 1e9), axis=1, keepdims=True)
    el2 = jnp.where(lanef == i1, -jnp.inf, el)
    t2 = jnp.max(el2, axis=1, keepdims=True)
    i2 = jnp.min(jnp.where(el2 == t2, lanef, 1e9), axis=1, keepdims=True)
    e2 = jnp.exp(t2 - t1)
    w1 = (1.0 / (1.0 + e2)) * p_grp
    w2 = (e2 / (1.0 + e2)) * p_grp
    if as_list:
        comb_ref[...] = jnp.where(lane == 0, i1 - e0, jnp.where(lane == 1, i2 - e0,
                                  jnp.where(lane == 2, w1, jnp.where(lane == 3, w2, 0.0))))
    else:
        comb_ref[...] = jnp.where(lanef == i1, w1, jnp.where(lanef == i2, w2, 0.0))


def moe_route(x, g_ffn, w_rg, b_rg, w_re, b_re, tm, as_list):
    t, d = x.shape
    tm = min(tm, t)
    assert EXP_PER_GROUP == 4 and w_rg.shape[1] == N_EXP_GROUPS and w_re.shape[1] == N_EXPERTS
    pad = LANES - N_EXP_GROUPS - N_EXPERTS
    wr = jnp.concatenate([w_rg, w_re, jnp.zeros((d, pad), F32)], axis=1)
    wr_hi = wr.astype(BF16)
    wr = jnp.concatenate([wr_hi, (wr - wr_hi.astype(F32)).astype(BF16)], axis=1)
    br = jnp.concatenate([b_rg, b_re, jnp.zeros((pad,), F32)]).reshape(1, LANES)
    return pl.pallas_call(
        functools.partial(_moe_route_kernel, as_list=as_list),
        out_shape=(jax.ShapeDtypeStruct((t, d), F32 if as_list else BF16),
                   jax.ShapeDtypeStruct((t, LANES), F32)),
        grid=(t // tm,),
        in_specs=[
            pl.BlockSpec((tm, d), lambda i: (i, 0)),
            pl.BlockSpec((1, d), lambda i: (0, 0)),
            pl.BlockSpec((d, 2 * LANES), lambda i: (0, 0)),
            pl.BlockSpec((1, LANES), lambda i: (0, 0)),
        ],
        out_specs=(pl.BlockSpec((tm, d), lambda i: (i, 0)), pl.BlockSpec((tm, LANES), lambda i: (i, 0))),
        compiler_params=_params("parallel"),
        name="moe_route",
    )(x, g_ffn.reshape(1, d), wr, br)


def _moe_experts_kernel(xn_ref, comb_ref, x_ref, wg_ref, wu_ref, wd_ref, gf_ref, o_ref, acc_ref):
    e = pl.program_id(1)
    lane = lax.broadcasted_iota(jnp.int32, comb_ref.shape, 1)
    c = jnp.sum(jnp.where(lane == e + ROUTER_EXPERT_LANE0, comb_ref[...], 0.0), axis=1, keepdims=True)
    xn = xn_ref[...]
    g = jnp.dot(xn, wg_ref[...].astype(BF16), preferred_element_type=F32)
    u = jnp.dot(xn, wu_ref[...].astype(BF16), preferred_element_type=F32)
    h = (g * _sigmoid(g)) * u
    part = jnp.dot((h * c).astype(BF16), wd_ref[...].astype(BF16), preferred_element_type=F32)

    @pl.when(e == 0)
    def _():
        acc_ref[...] = part

    @pl.when(e != 0)
    def _():
        acc_ref[...] += part

    @pl.when(e == pl.num_programs(1) - 1)
    def _():
        o_ref[...] = _rms_rows(x_ref[...] + acc_ref[...], gf_ref[...])


def moe_experts(xn, comb, x, w_gate, w_up, w_down, g_final, tm):
    t, d = x.shape
    n_exp, _, f = w_gate.shape
    tm = min(tm, t)
    return pl.pallas_call(
        _moe_experts_kernel,
        out_shape=jax.ShapeDtypeStruct((t, d), F32),
        grid=(t // tm, n_exp),
        in_specs=[
            pl.BlockSpec((tm, d), lambda i, e: (i, 0)),
            pl.BlockSpec((tm, LANES), lambda i, e: (i, 0)),
            pl.BlockSpec((tm, d), lambda i, e: (i, 0)),
            pl.BlockSpec((None, d, f), lambda i, e: (e, 0, 0)),
            pl.BlockSpec((None, d, f), lambda i, e: (e, 0, 0)),
            pl.BlockSpec((None, f, d), lambda i, e: (e, 0, 0)),
            pl.BlockSpec((1, d), lambda i, e: (0, 0)),
        ],
        out_specs=pl.BlockSpec((tm, d), lambda i, e: (i, 0)),
        scratch_shapes=[pltpu.VMEM((tm, d), F32)],
        compiler_params=_params("parallel", "arbitrary"),
        name="moe_experts",
    )(xn, comb, x, w_gate, w_up, w_down, g_final.reshape(1, d))


def _dispatch_plan(e_tok, tm):
    n_pairs = e_tok.size
    n_exp = N_EXPERTS
    i32 = jnp.int32
    ef = e_tok.reshape(-1)
    order = jnp.argsort(ef, stable=True).astype(i32)
    inv = jnp.argsort(order).astype(i32)
    sorted_e = ef[order]
    experts = jnp.arange(n_exp, dtype=i32)
    c_start = jnp.sum((sorted_e[None, :] < experts[:, None]).astype(i32), axis=1)
    counts = jnp.concatenate([c_start[1:], jnp.full((1,), n_pairs, i32)]) - c_start
    padded = ((counts + tm - 1) // tm) * tm
    g_end = jnp.cumsum(padded).astype(i32)
    g_start = g_end - padded
    shift = g_start - c_start
    pair_row = inv + shift[ef]
    n_rows = n_pairs + n_exp * tm
    n_tiles = n_rows // tm
    tile_start = jnp.arange(n_tiles, dtype=i32) * tm
    tile_expert = jnp.minimum(jnp.sum((g_end[None, :] <= tile_start[:, None]).astype(i32), axis=1),
                              n_exp - 1)
    row = jnp.arange(n_rows, dtype=i32)
    s_pos = row - jnp.repeat(shift[tile_expert], tm)
    row_token = order[jnp.clip(s_pos, 0, n_pairs - 1)] // 2
    return row_token.astype(i32), tile_expert.astype(i32), pair_row.astype(i32)


def _row_gather(src_hbm, dst, sem, row_ids, base, n_rows, slot):
    def body(r, carry):
        tok = row_ids[base + r]
        pltpu.make_async_copy(src_hbm.at[pl.ds(tok, 1), :], dst.at[slot, pl.ds(r, 1), :], sem.at[slot]).start()
        return carry
    lax.fori_loop(0, n_rows, body, 0, unroll=8)


def _row_gather_static(src_hbm, dst, sem, row_ids, base, n_rows, slot):
    for r in range(n_rows):
        tok = row_ids[base + r]
        pltpu.make_async_copy(src_hbm.at[pl.ds(tok, 1), :], dst.at[slot, pl.ds(r, 1), :], sem.at[slot]).start()


def _row_gather_wait(src_hbm, dst, sem, n_rows, slot):
    pltpu.make_async_copy(src_hbm.at[pl.ds(0, n_rows), :], dst.at[slot], sem.at[slot]).wait()


def _moe_sparse_kernel(tok_ref, texp_ref, xn_hbm, wg_ref, wu_ref, wd_ref, o_ref, xbuf, wgb, wub, wdb, sem):
    i = pl.program_id(0)
    n = pl.num_programs(0)
    tm = o_ref.shape[0]
    ahead = GATHER_SLOTS - 1

    @pl.when(i == 0)
    def _():
        for t in range(ahead):
            _row_gather(xn_hbm, xbuf, sem, tok_ref, jnp.minimum(t, n - 1) * tm, tm, t)

    @pl.when((i == 0) | (texp_ref[i] != texp_ref[jnp.maximum(i - 1, 0)]))
    def _():
        wgb[...] = wg_ref[...].astype(BF16)
        wub[...] = wu_ref[...].astype(BF16)
        wdb[...] = wd_ref[...].astype(BF16)

    slot = lax.rem(i, GATHER_SLOTS)
    _row_gather_wait(xn_hbm, xbuf, sem, tm, slot)
    xs = xbuf[slot].astype(BF16)
    g = jnp.dot(xs, wgb[...], preferred_element_type=F32)
    u = jnp.dot(xs, wub[...], preferred_element_type=F32)
    hb = ((g * _sigmoid(g)) * u).astype(BF16)
    o_ref[...] = jnp.dot(hb, wdb[...], preferred_element_type=F32)

    _row_gather_static(xn_hbm, xbuf, sem, tok_ref, jnp.minimum(i + ahead, n - 1) * tm, tm,
                       lax.rem(i + ahead, GATHER_SLOTS))

    @pl.when(i == n - 1)
    def _():
        for t in range(1, GATHER_SLOTS):
            _row_gather_wait(xn_hbm, xbuf, sem, tm, lax.rem(i + t, GATHER_SLOTS))


def moe_sparse_experts(xn, row_token, tile_expert, w_gate, w_up, w_down, tm):
    t, d = xn.shape
    n_exp, _, f = w_gate.shape
    n_rows = row_token.shape[0]
    return pl.pallas_call(
        _moe_sparse_kernel,
        out_shape=jax.ShapeDtypeStruct((n_rows, d), F32),
        grid_spec=pltpu.PrefetchScalarGridSpec(
            num_scalar_prefetch=2,
            grid=(n_rows // tm,),
            in_specs=[
                pl.BlockSpec(memory_space=pl.ANY),
                pl.BlockSpec((None, d, f), lambda i, tok, te: (te[i], 0, 0)),
                pl.BlockSpec((None, d, f), lambda i, tok, te: (te[i], 0, 0)),
                pl.BlockSpec((None, f, d), lambda i, tok, te: (te[i], 0, 0)),
            ],
            out_specs=pl.BlockSpec((tm, d), lambda i, tok, te: (i, 0)),
            scratch_shapes=[pltpu.VMEM((GATHER_SLOTS, tm, d), F32), pltpu.VMEM((d, f), BF16),
                            pltpu.VMEM((d, f), BF16), pltpu.VMEM((f, d), BF16),
                            pltpu.SemaphoreType.DMA((GATHER_SLOTS,))],
        ),
        compiler_params=_params("arbitrary"),
        name="moe_sparse_experts",
    )(row_token, tile_expert, xn, w_gate, w_up, w_down)


def _moe_combine_kernel(row_ref, ys_hbm, x_ref, route_ref, gf_ref, o_ref, ybuf, sem):
    i = pl.program_id(0)
    n = pl.num_programs(0)
    tm = o_ref.shape[0]
    ahead = GATHER_SLOTS - 1

    @pl.when(i == 0)
    def _():
        for t in range(ahead):
            _row_gather(ys_hbm, ybuf, sem, row_ref, jnp.minimum(t, n - 1) * 2 * tm, 2 * tm, t)

    slot = lax.rem(i, GATHER_SLOTS)
    _row_gather_wait(ys_hbm, ybuf, sem, 2 * tm, slot)
    k = TOP_K_IN_GROUP
    route = route_ref[...]
    lane = lax.broadcasted_iota(jnp.int32, route.shape, 1)
    w1 = jnp.sum(jnp.where(lane == k, route, 0.0), axis=1, keepdims=True)
    w2 = jnp.sum(jnp.where(lane == k + 1, route, 0.0), axis=1, keepdims=True)
    y = w1 * ybuf[slot, 0:tm, :] + w2 * ybuf[slot, tm:2 * tm, :]
    o_ref[...] = _rms_rows(x_ref[...] + y, gf_ref[...])

    _row_gather_static(ys_hbm, ybuf, sem, row_ref, jnp.minimum(i + ahead, n - 1) * 2 * tm, 2 * tm,
                       lax.rem(i + ahead, GATHER_SLOTS))

    @pl.when(i == n - 1)
    def _():
        for t in range(1, GATHER_SLOTS):
            _row_gather_wait(ys_hbm, ybuf, sem, 2 * tm, lax.rem(i + t, GATHER_SLOTS))


def moe_combine(ys, pair_row, x, route, g_final, tm):
    t, d = x.shape
    rows = jnp.transpose(pair_row.reshape(t // tm, tm, 2), (0, 2, 1)).reshape(-1)
    return pl.pallas_call(
        _moe_combine_kernel,
        out_shape=jax.ShapeDtypeStruct((t, d), F32),
        grid_spec=pltpu.PrefetchScalarGridSpec(
            num_scalar_prefetch=1,
            grid=(t // tm,),
            in_specs=[
                pl.BlockSpec(memory_space=pl.ANY),
                pl.BlockSpec((tm, d), lambda i, rw: (i, 0)),
                pl.BlockSpec((tm, LANES), lambda i, rw: (i, 0)),
                pl.BlockSpec((1, d), lambda i, rw: (0, 0)),
            ],
            out_specs=pl.BlockSpec((tm, d), lambda i, rw: (i, 0)),
            scratch_shapes=[pltpu.VMEM((GATHER_SLOTS, 2 * tm, d), F32),
                            pltpu.SemaphoreType.DMA((GATHER_SLOTS,))],
        ),
        compiler_params=_params("arbitrary"),
        name="moe_combine",
    )(rows, ys, x, route, g_final.reshape(1, d))


def kernel(x_prompt, x_sample, cache_k, cache_v, page_table, g_mix, w_in, g_sgu, w_sgu, b_sgu, w_proj_a, w_proj_b, w_out, g_ffn, w_router_group, b_router_group, w_router_expert, b_router_expert, w_exp_gate, w_exp_up, w_exp_down, g_final):
    depth = g_mix.shape[0]
    assert depth == 1
    layer = 0
    b_sz, s_len, d = x_prompt.shape
    n_seq, n_new, _ = x_sample.shape
    assert n_new == 1
    _, n_pool, page, n_heads, hd = cache_k.shape
    assert page == PAGE_SIZE and hd == HEAD_DIM
    a_w = n_heads * hd
    b_w = g_sgu.shape[1]
    n_pages = page_table.shape[1]
    past_len = n_pages * PAGE_SIZE
    assert past_len % MOBA_BLOCK == 0 and n_pages // PAGES_PER_BLOCK >= MOBA_TOPK

    w_in_l = w_in[layer]
    w_qkv = w_in_l[:, 0:3 * a_w].astype(BF16)
    w_q, w_k, w_v = (w_qkv[:, c * a_w:(c + 1) * a_w] for c in range(3))
    w_rest = w_in_l[:, 3 * a_w:].astype(BF16)
    w_pa = w_proj_a[layer].astype(BF16)
    w_pb = w_proj_b[layer].astype(BF16)
    w_o = w_out[layer].astype(BF16)
    w_eg, w_eu, w_ed = w_exp_gate[layer], w_exp_up[layer], w_exp_down[layer]

    def dense_tail(x2, att, rest, chunked):
        merged = sgu_merge(x2, att, rest, g_sgu[layer], w_sgu[layer], b_sgu[layer], w_pa, w_pb, w_o,
                           chunked=chunked, tm=512, tn=1024)
        x1, vn = (merged, None) if chunked else merged
        sparse = x1.shape[0] >= MOE_SPARSE_MIN_TOKENS
        xn, route = moe_route(x1, g_ffn[layer], w_router_group[layer], b_router_group[layer],
                              w_router_expert[layer], b_router_expert[layer], tm=256, as_list=sparse)
        if sparse:
            e_tok = route[:, 0:TOP_K_IN_GROUP].astype(jnp.int32)
            row_token, tile_expert, pair_row = _dispatch_plan(e_tok, MOE_ROW_TILE)
            ys = moe_sparse_experts(xn, row_token, tile_expert, w_eg, w_eu, w_ed, tm=MOE_ROW_TILE)
            y = moe_combine(ys, pair_row.reshape(-1, TOP_K_IN_GROUP), x1, route, g_final, tm=MOE_ROW_TILE)
        else:
            y = moe_experts(xn, route, x1, w_eg, w_eu, w_ed, g_final, tm=512)
        return y, vn

    xp = x_prompt.reshape(b_sz * s_len, d)
    g_in = g_mix[layer]
    q_p, k_p, v_p, kb_p, vb_p = norm_qkv(xp, g_in, w_qkv, tm=512)
    (rest_p,) = norm_matmul(xp, g_in, w_rest, (BF16,), tm=1024, tn=1024)
    pt_flat = page_table.reshape(-1)
    att_p, blk_sums = moba_prompt(q_p, kb_p, vb_p, block_means(k_p), b_sz, s_len, BF16,
                                  cache_k, layer, pt_flat)
    y_p, _ = dense_tail(xp, att_p, rest_p, chunked=True)

    xs = x_sample.reshape(n_seq, d)
    (q_s,) = norm_matmul(xs, g_in, w_q, (F32,), tm=n_seq, tn=1024)
    (k_s,) = norm_matmul(xs, g_in, w_k, (F32,), tm=n_seq, tn=1024)
    (v_s,) = norm_matmul(xs, g_in, w_v, (F32,), tm=n_seq, tn=1024)
    (rest_s,) = norm_matmul(xs, g_in, w_rest, (F32,), tm=n_seq, tn=1024)
    blk_sums = blk_sums.reshape(n_seq, n_pages // PAGES_PER_BLOCK, n_heads, hd)
    picks = block_choice(blk_sums, q_s)
    idx = jnp.transpose(picks[:, :MOBA_TOPK, :, 0], (0, 2, 1))
    pages = idx[..., None] * PAGES_PER_BLOCK + jnp.arange(PAGES_PER_BLOCK, dtype=jnp.int32)
    phys = jnp.take_along_axis(page_table[:, None, :], pages.reshape(n_seq, n_heads, -1), axis=2)
    att_s = sample_attend(q_s, k_s, v_s, cache_k, cache_v, layer, phys.reshape(-1), idx.reshape(-1),
                          past_len)
    y_s, vn_s = dense_tail(xs, att_s, rest_s, chunked=False)

    return (
        y_p.reshape(b_sz, s_len, d),
        y_s.reshape(n_seq, n_new, d),
        k_p.reshape(depth, b_sz, s_len, n_heads, hd),
        v_p.reshape(depth, b_sz, s_len, n_heads, hd),
        k_s.reshape(depth, n_seq, n_new, n_heads, hd),
        v_s.reshape(depth, n_seq, n_new, n_heads, hd),
        vn_s.reshape(depth, n_seq, n_new, b_w),
    )
```
